```python
import math
import jax, jax.numpy as jnp
from jax import lax
import numpy as np

D_MODEL = 1024
BATCH = 4
SEQ = 4096
DEPTH = 1

EPS = 1e-6
MLA_HEADS = 8
QK_NOPE = 64
QK_ROPE = 32
QK_HEAD = QK_NOPE + QK_ROPE
V_HEAD = 64
Q_LORA = 256
KV_LORA = 128
ROPE_THETA = 10000.0
Q_BLOCK = 128
RWKV_HEADS = 8
RWKV_HEAD = 64
RWKV_DIM = RWKV_HEADS * RWKV_HEAD
DECAY_LORA = 64
AAA_LORA = 64
GATE_LORA = 128
GN_EPS = 64e-5
PEER_HEADS = 8
N_KEYS = 128
N_EXPERTS = N_KEYS * N_KEYS
D_KEY = 256
HALF_KEY = D_KEY // 2
TOPK_HALF = 16
TOPK = 16
PEER_BLOCK = 64
N_BRANCH = 2
MLA_IN = Q_LORA + KV_LORA + QK_ROPE
RWKV_IN = 3 * RWKV_DIM + DECAY_LORA + AAA_LORA + GATE_LORA
GATE_IN = N_BRANCH * D_MODEL
IN_DIM = MLA_IN + RWKV_IN + GATE_IN
N_MOD = 6

kernel_name = "hybrid_mla_rwkv7_peer_adaln_block"


def rms_norm(x, g, eps=EPS):
    xf = x.astype(jnp.float32)
    y = xf * lax.rsqrt(jnp.mean(xf * xf, axis=-1, keepdims=True) + eps)
    return (y * g.astype(jnp.float32)).astype(x.dtype)


def modulate(h, shift, scale):
    return h * (1.0 + scale[:, None, :]) + shift[:, None, :]


def rope(x, positions):
    half = x.shape[-1] // 2
    inv_freq = ROPE_THETA ** (-jnp.arange(half, dtype=jnp.float32) / half)
    ang = positions.astype(jnp.float32)[..., None] * inv_freq
    cos = jnp.cos(ang)[:, :, None, :]
    sin = jnp.sin(ang)[:, :, None, :]
    xf = x.astype(jnp.float32)
    x1, x2 = xf[..., :half], xf[..., half:]
    return jnp.concatenate([x1 * cos - x2 * sin, x1 * sin + x2 * cos], axis=-1).astype(x.dtype)


def causal_block_attention(q, k, v):
    B, S, H, Dq = q.shape
    nb = S // Q_BLOCK
    scale = Dq ** -0.5
    qb = q.reshape(B, nb, Q_BLOCK, H, Dq).transpose(1, 0, 2, 3, 4)
    kpos = jnp.arange(S)

    def one_block(args):
        i, q_blk = args
        s = jnp.einsum('bqhd,bkhd->bhqk', q_blk, k).astype(jnp.float32) * scale
        qpos = i * Q_BLOCK + jnp.arange(Q_BLOCK)
        mask = kpos[None, :] <= qpos[:, None]
        s = jnp.where(mask[None, None], s, -jnp.inf)
        p = jax.nn.softmax(s, axis=-1).astype(v.dtype)
        return jnp.einsum('bhqk,bkhd->bqhd', p, v)

    out = lax.map(one_block, (jnp.arange(nb), qb))
    return out.transpose(1, 0, 2, 3, 4).reshape(B, S, H, v.shape[-1])


def mla_branch(q_a, kv_a, k_rope_raw, positions, q_a_norm, w_uq, kv_a_norm, w_ukv, q_norm, k_norm, w_o_mla):
    B, S, _ = q_a.shape
    cq = rms_norm(q_a, q_a_norm)
    q = (cq @ w_uq).reshape(B, S, MLA_HEADS, QK_HEAD)
    ckv = rms_norm(kv_a, kv_a_norm)
    kv = (ckv @ w_ukv).reshape(B, S, MLA_HEADS, QK_NOPE + V_HEAD)
    k_nope, v = kv[..., :QK_NOPE], kv[..., QK_NOPE:]
    q_nope = rms_norm(q[..., :QK_NOPE], q_norm[:QK_NOPE])
    q_pe = rope(rms_norm(q[..., QK_NOPE:], q_norm[QK_NOPE:]), positions)
    k_nope = rms_norm(k_nope, k_norm[:QK_NOPE])
    k_pe = rope(rms_norm(k_rope_raw[:, :, None, :], k_norm[QK_NOPE:]), positions)
    k_pe = jnp.broadcast_to(k_pe, (B, S, MLA_HEADS, QK_ROPE))
    qh = jnp.concatenate([q_nope, q_pe], axis=-1)
    kh = jnp.concatenate([k_nope, k_pe], axis=-1)
    o = causal_block_attention(qh, kh, v)
    return o.reshape(B, S, MLA_HEADS * V_HEAD) @ w_o_mla


def token_shift(p, mu):
    prev = jnp.pad(p, ((0, 0), (1, 0), (0, 0)))[:, :-1]
    return p + (prev - p) * mu


def rwkv7_scan(r, decay, k, v, kk, a):
    B, S, H, N = r.shape
    b = kk * a

    def step(state, inp):
        r_t, w_t, k_t, v_t, kk_t, b_t = inp
        sa = jnp.einsum('bhvk,bhk->bhv', state, -kk_t)
        state = (state * w_t[:, :, None, :] + sa[..., None] * b_t[:, :, None, :]
                 + v_t[..., None] * k_t[:, :, None, :])
        y = jnp.einsum('bhvk,bhk->bhv', state, r_t)
        return state, y

    xs = tuple(t.transpose(1, 0, 2, 3) for t in (r, decay, k, v, kk, b))
    s0 = jnp.zeros((B, H, N, N), jnp.float32)
    _, y = lax.scan(step, s0, xs)
    return y.transpose(1, 0, 2, 3)


def rwkv7_branch(p, mu_shift, w_decay_up, decay_base, w_aaa_up, aaa_base, w_gate_up,
                 k_k, k_a, r_k, ln_x_w, ln_x_b, w_o_rwkv):
    B, S, _ = p.shape
    p = token_shift(p, mu_shift)
    splits = [RWKV_DIM, 2 * RWKV_DIM, 3 * RWKV_DIM, 3 * RWKV_DIM + DECAY_LORA,
              3 * RWKV_DIM + DECAY_LORA + AAA_LORA]
    r, k, v, wd, ad, gd = jnp.split(p, splits, axis=-1)
    w = -jax.nn.softplus(-(decay_base + jnp.tanh(wd) @ w_decay_up)) - 0.5
    decay = jnp.exp(-jnp.exp(w.astype(jnp.float32)))
    a = jax.nn.sigmoid(aaa_base + ad @ w_aaa_up)
    g = jax.nn.sigmoid(gd) @ w_gate_up

    def heads(t):
        return t.reshape(B, S, RWKV_HEADS, RWKV_HEAD).astype(jnp.float32)

    kk = heads(k * k_k)
    kk = kk / jnp.maximum(jnp.linalg.norm(kk, axis=-1, keepdims=True), 1e-12)
    a_h = heads(a)
    k_h = heads(k) * (1.0 + (a_h - 1.0) * k_a.reshape(RWKV_HEADS, RWKV_HEAD).astype(jnp.float32))
    r_h, v_h = heads(r), heads(v)
    y = rwkv7_scan(r_h, heads(decay), k_h, v_h, kk, a_h)
    mu = jnp.mean(y, axis=-1, keepdims=True)
    var = jnp.mean(jnp.square(y - mu), axis=-1, keepdims=True)
    y = (y - mu) * lax.rsqrt(var + GN_EPS)
    y = y * ln_x_w.reshape(RWKV_HEADS, RWKV_HEAD) + ln_x_b.reshape(RWKV_HEADS, RWKV_HEAD)
    y = y + jnp.sum(r_h * k_h * r_k, axis=-1, keepdims=True) * v_h
    y = y.reshape(B, S, RWKV_DIM).astype(p.dtype) * g
    return y @ w_o_rwkv


def token_mixer(h, positions, w_in, mu_shift, q_a_norm, w_uq, kv_a_norm, w_ukv, q_norm, k_norm, w_o_mla,
                w_decay_up, decay_base, w_aaa_up, aaa_base, w_gate_up, k_k, k_a, r_k, ln_x_w, ln_x_b,
                w_o_rwkv, w_out):
    proj = h @ w_in
    q_a, kv_a, k_rope_raw, rwkv_in, gate_logits = jnp.split(
        proj, [Q_LORA, Q_LORA + KV_LORA, MLA_IN, MLA_IN + RWKV_IN], axis=-1)
    o_mla = mla_branch(q_a, kv_a, k_rope_raw, positions, q_a_norm, w_uq, kv_a_norm, w_ukv,
                       q_norm, k_norm, w_o_mla)
    o_rwkv = rwkv7_branch(rwkv_in, mu_shift, w_decay_up, decay_base, w_aaa_up, aaa_base, w_gate_up,
                          k_k, k_a, r_k, ln_x_w, ln_x_b, w_o_rwkv)
    g_mla, g_rwkv = jnp.split(jax.nn.sigmoid(gate_logits), 2, axis=-1)
    return (g_mla * o_mla + g_rwkv * o_rwkv) @ w_out


def peer_ffn(h, w_query, sub_keys1, sub_keys2, expert_down, expert_up):
    B, S, D = h.shape
    q = (h @ w_query).reshape(B, S, PEER_HEADS, 2, HALF_KEY)
    s1 = jnp.einsum('bshd,nd->bshn', q[..., 0, :], sub_keys1).astype(jnp.float32)
    s2 = jnp.einsum('bshd,nd->bshn', q[..., 1, :], sub_keys2).astype(jnp.float32)
    v1, i1 = lax.top_k(s1, TOPK_HALF)
    v2, i2 = lax.top_k(s2, TOPK_HALF)
    cand = (v1[..., :, None] + v2[..., None, :]).reshape(B, S, PEER_HEADS, TOPK_HALF * TOPK_HALF)
    cand_idx = (i1[..., :, None] * N_KEYS + i2[..., None, :]).reshape(B, S, PEER_HEADS, TOPK_HALF * TOPK_HALF)
    top_v, pos = lax.top_k(cand, TOPK)
    expert_idx = jnp.take_along_axis(cand_idx, pos, axis=-1)
    gates = jax.nn.softmax(top_v, axis=-1).astype(h.dtype)
    nb = S // PEER_BLOCK
    hb = h.reshape(B, nb, PEER_BLOCK, D).transpose(1, 0, 2, 3)
    idxb = expert_idx.reshape(B, nb, PEER_BLOCK, PEER_HEADS, TOPK).transpose(1, 0, 2, 3, 4)
    gb = gates.reshape(B, nb, PEER_BLOCK, PEER_HEADS, TOPK).transpose(1, 0, 2, 3, 4)

    def one_block(args):
        h_blk, idx_blk, g_blk = args
        u = jnp.take(expert_down, idx_blk, axis=0)
        act = jax.nn.gelu(jnp.einsum('bpd,bphkd->bphk', h_blk, u))
        vv = jnp.take(expert_up, idx_blk, axis=0)
        return jnp.einsum('bphk,bphkd->bpd', g_blk * act, vv)

    out = lax.map(one_block, (hb, idxb, gb))
    return out.transpose(1, 0, 2, 3).reshape(B, S, D)


def setup_inputs(seed: int = 0) -> dict:
    key = jax.random.key(seed)
    ks = jax.random.split(key, 40)
    f32 = jnp.float32

    def nrm(k, shape, scale):
        return jax.random.normal(k, (DEPTH,) + shape, f32) * scale

    def gain(k, shape):
        return 1.0 + 0.1 * jax.random.normal(k, (DEPTH,) + shape, f32)

    x = jax.random.normal(ks[0], (BATCH, SEQ, D_MODEL), f32)
    c = jax.random.normal(ks[1], (BATCH, D_MODEL), f32)
    offsets = jax.random.randint(ks[2], (BATCH, 1), 0, 1024, dtype=jnp.int32)
    positions = offsets + jnp.arange(SEQ, dtype=jnp.int32)[None, :]
    return {
        "x": x, "c": c, "positions": positions,
        "w_ada": nrm(ks[3], (D_MODEL, N_MOD * D_MODEL), 0.3 * D_MODEL ** -0.5),
        "b_ada": nrm(ks[4], (N_MOD * D_MODEL,), 0.05),
        "norm_mix": gain(ks[5], (D_MODEL,)),
        "w_in": nrm(ks[6], (D_MODEL, IN_DIM), D_MODEL ** -0.5),
        "mu_shift": jax.random.uniform(ks[7], (DEPTH, RWKV_IN), f32),
        "q_a_norm": gain(ks[8], (Q_LORA,)),
        "w_uq": nrm(ks[9], (Q_LORA, MLA_HEADS * QK_HEAD), Q_LORA ** -0.5),
        "kv_a_norm": gain(ks[10], (KV_LORA,)),
        "w_ukv": nrm(ks[11], (KV_LORA, MLA_HEADS * (QK_NOPE + V_HEAD)), KV_LORA ** -0.5),
        "q_norm": gain(ks[12], (QK_HEAD,)),
        "k_norm": gain(ks[13], (QK_HEAD,)),
        "w_o_mla": nrm(ks[14], (MLA_HEADS * V_HEAD, D_MODEL), (MLA_HEADS * V_HEAD) ** -0.5),
        "w_decay_up": nrm(ks[15], (DECAY_LORA, RWKV_DIM), 0.1),
        "decay_base": jax.random.uniform(ks[16], (DEPTH, RWKV_DIM), f32, -6.0, 2.0),
        "w_aaa_up": nrm(ks[17], (AAA_LORA, RWKV_DIM), 0.1),
        "aaa_base": nrm(ks[18], (RWKV_DIM,), 0.5),
        "w_gate_up": nrm(ks[19], (GATE_LORA, RWKV_DIM), GATE_LORA ** -0.5),
        "k_k": 0.85 + nrm(ks[20], (RWKV_DIM,), 0.1),
        "k_a": gain(ks[21], (RWKV_DIM,)),
        "r_k": nrm(ks[22], (RWKV_HEADS, RWKV_HEAD), 0.1),
        "ln_x_w": gain(ks[23], (RWKV_DIM,)),
        "ln_x_b": nrm(ks[24], (RWKV_DIM,), 0.02),
        "w_o_rwkv": nrm(ks[25], (RWKV_DIM, D_MODEL), RWKV_DIM ** -0.5),
        "w_out": nrm(ks[26], (D_MODEL, D_MODEL), D_MODEL ** -0.5),
        "norm_ffn": gain(ks[27], (D_MODEL,)),
        "w_query": nrm(ks[28], (D_MODEL, PEER_HEADS * D_KEY), D_MODEL ** -0.5),
        "sub_keys1": nrm(ks[29], (N_KEYS, HALF_KEY), HALF_KEY ** -0.5),
        "sub_keys2": nrm(ks[30], (N_KEYS, HALF_KEY), HALF_KEY ** -0.5),
        "expert_down": nrm(ks[31], (N_EXPERTS, D_MODEL), D_MODEL ** -0.5),
        "expert_up": nrm(ks[32], (N_EXPERTS, D_MODEL), PEER_HEADS ** -0.5),
    }


def reference(x, c, positions, w_ada, b_ada, norm_mix, w_in, mu_shift, q_a_norm, w_uq, kv_a_norm, w_ukv,
              q_norm, k_norm, w_o_mla, w_decay_up, decay_base, w_aaa_up, aaa_base, w_gate_up, k_k, k_a, r_k,
              ln_x_w, ln_x_b, w_o_rwkv, w_out, norm_ffn, w_query, sub_keys1, sub_keys2, expert_down, expert_up):
    for l in range(DEPTH):
        mod = jax.nn.silu(c) @ w_ada[l] + b_ada[l]
        sh1, sc1, gt1, sh2, sc2, gt2 = jnp.split(mod, N_MOD, axis=-1)
        h = modulate(rms_norm(x, norm_mix[l]), sh1, sc1)
        mix = token_mixer(h, positions, w_in[l], mu_shift[l], q_a_norm[l], w_uq[l], kv_a_norm[l], w_ukv[l],
                          q_norm[l], k_norm[l], w_o_mla[l], w_decay_up[l], decay_base[l], w_aaa_up[l],
                          aaa_base[l], w_gate_up[l], k_k[l], k_a[l], r_k[l], ln_x_w[l], ln_x_b[l],
                          w_o_rwkv[l], w_out[l])
        x = x + gt1[:, None, :] * mix
        h = modulate(rms_norm(x, norm_ffn[l]), sh2, sc2)
        x = x + gt2[:, None, :] * peer_ffn(h, w_query[l], sub_keys1[l], sub_keys2[l], expert_down[l], expert_up[l])
    return x
```

```python
import functools

import jax
import jax.numpy as jnp
from jax import lax
from jax.experimental import pallas as pl
from jax.experimental.pallas import tpu as pltpu

F32 = jnp.float32
BF16 = jnp.bfloat16

EPS = 1e-6
GN_EPS = 64e-5
ROPE_THETA = 10000.0
N_MOD = 6
MLA_HEADS = 8
QK_NOPE = 64
QK_ROPE = 32
V_HEAD = 64
Q_LORA = 256
KV_LORA = 128
RWKV_HEADS = 8
RWKV_HEAD = 64
RWKV_DIM = RWKV_HEADS * RWKV_HEAD
DECAY_LORA = 64
AAA_LORA = 64
GATE_LORA = 128
PEER_HEADS = 8
N_KEYS = 128
TOPK = 16

LANES = 128
VMEM_LIMIT = 56 * 1024 * 1024
NEG = -3.0e38

NN = (((1,), (0,)), ((), ()))
NT = (((1,), (1,)), ((), ()))


def _dg(a, b, dims=NN):
    return lax.dot_general(a, b, dims, preferred_element_type=F32)


def _mm(a, b, dims=NN):
    return _dg(a.astype(BF16), b.astype(BF16), dims)


def _split(a):
    hi = a.astype(BF16)
    lo = (a - hi.astype(F32)).astype(BF16)
    return hi, lo


def _mm3(a, b, dims=NN):
    ah, al = _split(a)
    bh, bl = _split(b)
    return _dg(ah, bh, dims) + (_dg(ah, bl, dims) + _dg(al, bh, dims))


def _mm_lx(a, b_exact, dims=NN):
    ah, al = _split(a)
    return _dg(ah, b_exact, dims) + _dg(al, b_exact, dims)


def _params(sem):
    return pltpu.CompilerParams(dimension_semantics=sem, vmem_limit_bytes=VMEM_LIMIT)


def _full(shape):
    nd = len(shape)
    return pl.BlockSpec(shape, lambda *_: (0,) * nd)


def _ada_kernel(c_ref, w_ref, b_ref, o_ref):
    c = c_ref[...]
    s = c * jax.nn.sigmoid(c)
    o_ref[...] = _mm3(s, w_ref[...]) + b_ref[...]


def _ada(c8, w_ada, b_ada):
    d = c8.shape[1]
    n = w_ada.shape[1]
    bn = 1024
    return pl.pallas_call(
        _ada_kernel,
        grid=(n // bn,),
        in_specs=[_full(c8.shape),
                  pl.BlockSpec((d, bn), lambda j: (0, j)),
                  pl.BlockSpec((1, bn), lambda j: (0, j))],
        out_specs=pl.BlockSpec((c8.shape[0], bn), lambda j: (0, j)),
        out_shape=jax.ShapeDtypeStruct((c8.shape[0], n), F32),
        compiler_params=_params(("arbitrary",)),
        name="ada",
    )(c8, w_ada, b_ada.reshape(1, n))


_SEG = (0, Q_LORA, Q_LORA + KV_LORA, Q_LORA + KV_LORA + LANES)
RWKV_IN = 3 * RWKV_DIM + DECAY_LORA + AAA_LORA + GATE_LORA


def _norm_mod(x, g, shift, scale):
    ms = jnp.mean(x * x, axis=-1, keepdims=True)
    h = x * lax.rsqrt(ms + EPS) * g
    return h * (1.0 + scale) + shift


def _inproj_kernel(x_ref, mod_ref, g_ref, w_ref, qa_ref, kva_ref, kr_ref, rw_ref, gate_ref):
    h = _norm_mod(x_ref[...], g_ref[...], mod_ref[0, 0:1, :], mod_ref[0, 1:2, :])
    hb = h.astype(BF16)
    o0, o1, o2, o3 = _SEG
    o4 = o3 + RWKV_IN
    qa_ref[...] = _dg(hb, w_ref[:, o0:o1])
    kva_ref[...] = _dg(hb, w_ref[:, o1:o2])
    kr_ref[...] = _dg(hb, w_ref[:, o2:o3])
    rw_ref[...] = _dg(hb, w_ref[:, o3:o4])
    gate_ref[...] = jax.nn.sigmoid(_dg(hb, w_ref[:, o4:])).astype(BF16)


def _inproj(x2, mod, norm_mix, w_cat, seq, tm):
    t, d = x2.shape
    per = seq // tm
    n_gate = w_cat.shape[1] - _SEG[3] - RWKV_IN
    row = lambda w: pl.BlockSpec((tm, w), lambda i: (i, 0))
    return pl.pallas_call(
        _inproj_kernel,
        grid=(t // tm,),
        in_specs=[row(d),
                  pl.BlockSpec((1, 8, d), lambda i: (i // per, 0, 0)),
                  _full((1, d)),
                  _full(w_cat.shape)],
        out_specs=[row(Q_LORA), row(KV_LORA), row(LANES), row(RWKV_IN), row(n_gate)],
        out_shape=[jax.ShapeDtypeStruct((t, Q_LORA), F32),
                   jax.ShapeDtypeStruct((t, KV_LORA), F32),
                   jax.ShapeDtypeStruct((t, LANES), F32),
                   jax.ShapeDtypeStruct((t, RWKV_IN), F32),
                   jax.ShapeDtypeStruct((t, n_gate), BF16)],
        compiler_params=_params(("arbitrary",)),
        name="inproj",
    )(x2, mod, norm_mix.reshape(1, d), w_cat)


def _mlaprep_kernel(qa_ref, kva_ref, kr_ref, pos_ref, qan_ref, kvan_ref, wuq_ref, wuk_ref, wuv_ref,
                    qg_ref, kg_ref, freq_ref, ind_ref, rot_ref, q_ref, k_ref, v_ref, *, scale):
    ind = ind_ref[...]
    rot = rot_ref[...]
    ang = pos_ref[...].astype(F32) * freq_ref[...]
    cos = jnp.cos(ang)
    sin = jnp.sin(ang)

    def rms(a, g):
        return a * lax.rsqrt(jnp.mean(a * a, axis=-1, keepdims=True) + EPS) * g

    def head_norm(slab, g):
        ms = _mm_lx(slab * slab, ind)
        return slab * lax.rsqrt(ms + EPS) * g

    def rope(a):
        return a * cos + _dg(a.astype(BF16), rot) * sin

    cq = rms(qa_ref[...], qan_ref[...]).astype(BF16)
    ckv = rms(kva_ref[...], kvan_ref[...]).astype(BF16)
    qg = qg_ref[...]
    kg = kg_ref[...]
    k_pe = rope(head_norm(kr_ref[...], kg))
    for h in range(MLA_HEADS):
        sl = slice(h * LANES, (h + 1) * LANES)
        qh = rope(head_norm(_dg(cq, wuq_ref[:, sl]), qg)) * scale
        q_ref[0, h] = qh.astype(BF16)
        kh = head_norm(_dg(ckv, wuk_ref[:, sl]), kg) + k_pe
        k_ref[0, h] = kh.astype(BF16)
        v_ref[0, h] = _dg(ckv, wuv_ref[:, sl]).astype(BF16)


def _mlaprep(qa, kva, kr, pos, consts, batch, seq, tm):
    per = seq // tm
    row = lambda w: pl.BlockSpec((tm, w), lambda b, s: (b * per + s, 0))
    hm = pl.BlockSpec((1, MLA_HEADS, tm, LANES), lambda b, s: (b, 0, s, 0))
    hshape = jax.ShapeDtypeStruct((batch, MLA_HEADS, seq, LANES), BF16)
    return pl.pallas_call(
        functools.partial(_mlaprep_kernel, scale=float((QK_NOPE + QK_ROPE) ** -0.5)),
        grid=(batch, per),
        in_specs=[row(Q_LORA), row(KV_LORA), row(LANES), row(1)] + [_full(c.shape) for c in consts],
        out_specs=[hm, hm, hm],
        out_shape=[hshape, hshape, hshape],
        compiler_params=_params(("arbitrary", "arbitrary")),
        name="mlaprep",
    )(qa, kva, kr, pos, *consts)


def _attn_kernel(q_ref, k_ref, v_ref, o_ref, *, tq):
    qi = pl.program_id(2)
    row = lax.broadcasted_iota(jnp.int32, (tq, tq), 0)
    col = lax.broadcasted_iota(jnp.int32, (tq, tq), 1)
    causal = col <= row
    out = jnp.zeros((tq, LANES), F32)
    for hh in range(2):
        q = q_ref[0, hh]

        def block(j, carry, masked):
            m, l, acc = carry
            k = k_ref[0, hh, pl.ds(pl.multiple_of(j * tq, tq), tq), :]
            v = v_ref[0, hh, pl.ds(pl.multiple_of(j * tq, tq), tq), :]
            s = _dg(q, k, NT)
            if masked:
                s = jnp.where(causal, s, NEG)
            m_new = jnp.maximum(m, jnp.max(s, axis=-1, keepdims=True))
            alpha = jnp.exp(m - m_new)
            p = jnp.exp(s - m_new)
            l = alpha * l + jnp.sum(p, axis=-1, keepdims=True)
            acc = alpha * acc + _dg(p.astype(BF16), v)
            return m_new, l, acc

        init = (jnp.full((tq, 1), NEG, F32), jnp.zeros((tq, 1), F32), jnp.zeros((tq, LANES), F32))
        carry = lax.fori_loop(0, qi, lambda j, c: block(j, c, False), init)
        m, l, acc = block(qi, carry, True)
        out = out + acc / l
    o_ref[0] = out.astype(BF16)


def _attention(q, k, v, tq):
    batch, heads, seq, _ = q.shape
    return pl.pallas_call(
        functools.partial(_attn_kernel, tq=tq),
        grid=(batch, heads // 2, seq // tq),
        in_specs=[pl.BlockSpec((1, 2, tq, LANES), lambda b, p, i: (b, p, i, 0)),
                  pl.BlockSpec((1, 2, seq, LANES), lambda b, p, i: (b, p, 0, 0)),
                  pl.BlockSpec((1, 2, seq, LANES), lambda b, p, i: (b, p, 0, 0))],
        out_specs=pl.BlockSpec((1, tq, LANES), lambda b, p, i: (b, i, p)),
        out_shape=jax.ShapeDtypeStruct((batch, seq, heads // 2 * LANES), BF16),
        compiler_params=_params(("arbitrary", "arbitrary", "arbitrary")),
        name="attn",
    )(q, k, v)


def _softplus(z):
    return jnp.maximum(z, 0.0) + jnp.log(1.0 + jnp.exp(-jnp.abs(z)))


def _rwprep_kernel(p_ref, prev_ref, mu_ref, wd_ref, db_ref, wa_ref, ab_ref, wg_ref, kk_ref, ka_ref, seg_ref,
                   r_ref, lw_ref, k_ref, v_ref, kkn_ref, b_ref, g_ref):
    p = p_ref[...]
    tm = p.shape[0]
    first = pl.program_id(1) == 0
    last_prev = jnp.where(first, 0.0, prev_ref[7:8, :])
    rows = lax.broadcasted_iota(jnp.int32, p.shape, 0)
    prev = jnp.where(rows == 0, last_prev, pltpu.roll(p, 1, 0))
    p = p + (prev - p) * mu_ref[...]
    d = RWKV_DIM
    r = p[:, 0:d]
    k = p[:, d:2 * d]
    v = p[:, 2 * d:3 * d]
    o = 3 * d
    wd = p[:, o:o + DECAY_LORA]
    ad = p[:, o + DECAY_LORA:o + DECAY_LORA + AAA_LORA]
    gd = p[:, o + DECAY_LORA + AAA_LORA:]
    w = -_softplus(-(db_ref[...] + _mm3(jnp.tanh(wd), wd_ref[...]))) - 0.5
    a = jax.nn.sigmoid(ab_ref[...] + _mm3(ad, wa_ref[...]))
    kk = k * kk_ref[...]
    n2 = _mm_lx(kk * kk, seg_ref[...])
    kk = kk / jnp.maximum(jnp.sqrt(n2), 1e-12)
    r_ref[...] = r
    lw_ref[...] = -jnp.exp(w)
    k_ref[...] = k * (1.0 + (a - 1.0) * ka_ref[...])
    v_ref[...] = v
    kkn_ref[...] = kk
    b_ref[...] = kk * a
    g_ref[...] = _mm(jax.nn.sigmoid(gd), wg_ref[...])
    del tm


def _rwprep(rw, consts, batch, seq, tm):
    t = rw.shape[0]
    per = seq // tm
    row = pl.BlockSpec((tm, RWKV_DIM), lambda b, s: (b * per + s, 0))
    shp = jax.ShapeDtypeStruct((t, RWKV_DIM), F32)
    return pl.pallas_call(
        _rwprep_kernel,
        grid=(batch, per),
        in_specs=[pl.BlockSpec((tm, RWKV_IN), lambda b, s: (b * per + s, 0)),
                  pl.BlockSpec((8, RWKV_IN), lambda b, s: (jnp.maximum((b * per + s) * (tm // 8) - 1, 0), 0))]
                 + [_full(c.shape) for c in consts],
        out_specs=[row] * 7,
        out_shape=[shp] * 7,
        compiler_params=_params(("arbitrary", "arbitrary")),
        name="rwprep",
    )(rw, rw, *consts)


def _chunk_math(r, lw, k, v, kk, b, mm):
    c = r.shape[0]
    row = lax.broadcasted_iota(jnp.int32, (c, c), 0)
    col = lax.broadcasted_iota(jnp.int32, (c, c), 1)
    incl = row >= col
    strict = row > col
    tri = incl.astype(BF16)
    l_hi = lw.astype(BF16)
    l_r1 = lw - l_hi.astype(F32)
    l_mid = l_r1.astype(BF16)
    l_lo = (l_r1 - l_mid.astype(F32)).astype(BF16)
    cum = _dg(tri, l_hi) + (_dg(tri, l_mid) + _dg(tri, l_lo))
    tot = cum[c - 1:c, :]
    e_pos = jnp.exp(cum)
    e_neg = jnp.exp(-cum)
    e_rem = jnp.exp(tot - cum)
    kk_t = kk * jnp.exp(cum - lw)
    b_t = b * e_neg
    k_t = k * e_neg
    r_t = r * e_pos
    b_h = b * e_rem
    k_h = k * e_rem
    lane = lax.broadcasted_iota(jnp.int32, (1, LANES), 1)
    rhs = jnp.concatenate([b_t, k_t], axis=0)
    p1 = jnp.zeros((c, LANES), F32)
    p2 = jnp.zeros((c, LANES), F32)
    q1 = jnp.zeros((c, LANES), F32)
    q2 = jnp.zeros((c, LANES), F32)
    for hh in range(2):
        msk = ((lane >= 64 * hh) & (lane < 64 * (hh + 1))).astype(F32)
        kk_m = kk_t * msk
        r_m = r_t * msk
        big = mm(jnp.concatenate([kk_m, r_m], axis=0), rhs, NT)
        a_ab = jnp.where(strict, big[:c, :c], 0.0)
        a_ak = jnp.where(strict, big[:c, c:], 0.0)
        b_rb = jnp.where(incl, big[c:, :c], 0.0)
        b_rk = jnp.where(incl, big[c:, c:], 0.0)
        av = mm(jnp.concatenate([a_ak, b_rk], axis=0), v)
        z = jnp.concatenate([kk_m, av[:c]], axis=1)
        n = a_ab
        z = z - mm(n, z)
        steps = c.bit_length() - 2
        for _ in range(steps):
            n = mm(n, n)
            z = z + mm(n, z)
        p1h = z[:, :LANES]
        p2h = z[:, LANES:] * msk
        bp = mm(b_rb, jnp.concatenate([p1h, p2h], axis=1))
        p1 = p1 + p1h
        p2 = p2 + p2h
        q1 = q1 + (r_m - bp[:, :LANES])
        q2 = q2 + (av[c:] - bp[:, LANES:]) * msk
    r128 = lax.broadcasted_iota(jnp.int32, (LANES, LANES), 0)
    c128 = lax.broadcasted_iota(jnp.int32, (LANES, LANES), 1)
    same = (r128 >= 64) == (c128 >= 64)
    btp = mm(b_h.T, jnp.concatenate([p1, p2], axis=1))
    ktv = mm(k_h.T, v)
    g = jnp.where(r128 == c128, jnp.exp(tot), 0.0) - jnp.where(same, btp[:, :LANES], 0.0)
    hmat = jnp.where(same, ktv - btp[:, LANES:], 0.0)
    return q1, q2, g, hmat


def _rwchunk_kernel(r_ref, lw_ref, k_ref, v_ref, kk_ref, b_ref, q1_ref, q2_ref, g_ref, h_ref):
    q1, q2, g, hmat = _chunk_math(r_ref[...], lw_ref[...], k_ref[...], v_ref[...], kk_ref[...], b_ref[...], _mm3)
    q1_ref[...] = q1
    q2_ref[...] = q2
    g_ref[0, 0] = g
    h_ref[0, 0] = hmat


def _rwchunk(r, lw, k, v, kk, b, c):
    t = r.shape[0]
    pairs = RWKV_DIM // LANES
    blk = pl.BlockSpec((c, LANES), lambda i, p: (i, p))
    sq = pl.BlockSpec((1, 1, LANES, LANES), lambda i, p: (i, p, 0, 0))
    return pl.pallas_call(
        _rwchunk_kernel,
        grid=(t // c, pairs),
        in_specs=[blk] * 6,
        out_specs=[blk, blk, sq, sq],
        out_shape=[jax.ShapeDtypeStruct((t, RWKV_DIM), F32)] * 2
                  + [jax.ShapeDtypeStruct((t // c, pairs, LANES, LANES), F32)] * 2,
        compiler_params=_params(("arbitrary", "arbitrary")),
        name="rwchunk",
    )(r, lw, k, v, kk, b)


def _rwscan_kernel(q1_ref, q2_ref, g_ref, h_ref, y_ref, s_ref):
    @pl.when(pl.program_id(1) == 0)
    def _():
        s_ref[...] = jnp.zeros_like(s_ref)

    for p in range(RWKV_DIM // LANES):
        sl = slice(p * LANES, (p + 1) * LANES)
        s = s_ref[p]
        y_ref[:, sl] = _mm3(q1_ref[:, sl], s) + q2_ref[:, sl]
        s_ref[p] = _mm3(g_ref[0, p], s) + h_ref[0, p]


def _rwscan(q1, q2, g, hmat, batch, seq, c):
    t = q1.shape[0]
    per = seq // c
    pairs = RWKV_DIM // LANES
    blk = pl.BlockSpec((c, RWKV_DIM), lambda b, i: (b * per + i, 0))
    sq = pl.BlockSpec((1, pairs, LANES, LANES), lambda b, i: (b * per + i, 0, 0, 0))
    return pl.pallas_call(
        _rwscan_kernel,
        grid=(batch, per),
        in_specs=[blk, blk, sq, sq],
        out_specs=blk,
        out_shape=jax.ShapeDtypeStruct((t, RWKV_DIM), F32),
        scratch_shapes=[pltpu.VMEM((pairs, LANES, LANES), F32)],
        compiler_params=_params(("arbitrary", "arbitrary")),
        name="rwscan",
    )(q1, q2, g, hmat)


def _post_kernel(y_ref, r_ref, k_ref, v_ref, g_ref, om_ref, gate_ref, x_ref, mod_ref, rk_ref, lnw_ref, lnb_ref,
                 segm_ref, wom_ref, wor_ref, wout_ref, nf_ref, x1_ref, h2_ref, h2b_ref):
    segm = segm_ref[...]
    y = y_ref[...]
    mu = _mm_lx(y, segm)
    dv = y - mu
    var = _mm_lx(dv * dv, segm)
    yn = dv * lax.rsqrt(var + GN_EPS) * lnw_ref[...] + lnb_ref[...]
    bonus = _mm_lx(r_ref[...] * k_ref[...] * rk_ref[...], segm) * float(RWKV_HEAD)
    yy = (yn + bonus * v_ref[...]) * g_ref[...]
    o_r = _mm(yy, wor_ref[...])
    o_m = _dg(om_ref[0], wom_ref[...])
    d = o_m.shape[1]
    gm = gate_ref[:, 0:d].astype(F32)
    gr = gate_ref[:, d:2 * d].astype(F32)
    mix = _mm(gm * o_m + gr * o_r, wout_ref[...])
    x1 = x_ref[...] + mod_ref[0, 2:3, :] * mix
    x1_ref[...] = x1
    h2 = _norm_mod(x1, nf_ref[...], mod_ref[0, 3:4, :], mod_ref[0, 4:5, :])
    h2_ref[...] = h2
    h2b_ref[...] = h2.astype(BF16)


def _post(y, r, k, v, g, o_mla, gates, x2, mod, consts, batch, seq, tm):
    t, d = x2.shape
    per = seq // tm
    row = lambda w: pl.BlockSpec((tm, w), lambda b, s: (b * per + s, 0))
    return pl.pallas_call(
        _post_kernel,
        grid=(batch, per),
        in_specs=[row(RWKV_DIM)] * 5
                 + [pl.BlockSpec((1, tm, o_mla.shape[2]), lambda b, s: (b, s, 0)),
                    row(gates.shape[1]), row(d),
                    pl.BlockSpec((1, 8, d), lambda b, s: (b, 0, 0))]
                 + [_full(c.shape) for c in consts],
        out_specs=[row(d), row(d), row(d)],
        out_shape=[jax.ShapeDtypeStruct((t, d), F32), jax.ShapeDtypeStruct((t, d), F32),
                   jax.ShapeDtypeStruct((t, d), BF16)],
        compiler_params=_params(("arbitrary", "arbitrary")),
        name="post",
    )(y, r, k, v, g, o_mla, gates, x2, mod, *consts)


def _cand_pairs(n):
    return [(a, b) for a in range(n) for b in range(n) if (a + 1) * (b + 1) <= n]


def _top_values(s, n):
    vals = []
    cur = s
    for i in range(n):
        m = jnp.max(cur, axis=0, keepdims=True)
        vals.append(m)
        if i + 1 < n:
            cur = jnp.where(cur >= m, NEG, cur)
    return vals


def _route_kernel(h_ref, wqh_ref, wql_ref, k1_ref, k2_ref, th_ref, e1_ref, s2_ref, e2_ref):
    hh, hl = _split(h_ref[...])
    q = _dg(wqh_ref[...], hh, NT) + (_dg(wqh_ref[...], hl, NT) + _dg(wql_ref[...], hh, NT))
    k1 = k1_ref[...]
    k2 = k2_ref[...]
    nk = TOPK + 1
    pairs = _cand_pairs(nk)
    for h in range(PEER_HEADS):
        base = 2 * N_KEYS * h
        s1 = _mm3(k1, q[base:base + N_KEYS])
        s2 = _mm3(k2, q[base + N_KEYS:base + 2 * N_KEYS])
        v1 = _top_values(s1, nk)
        v2 = _top_values(s2, nk)
        pad = [jnp.full_like(v1[0], NEG)] * (-len(pairs) % 8)
        cand = jnp.concatenate([v1[a] + v2[b] for a, b in pairs] + pad, axis=0)
        top = _top_values(cand, nk)
        z = sum(jnp.exp(top[i] - top[0]) for i in range(1, TOPK)) + 1.0
        tau = 0.5 * (top[TOPK - 1] + top[TOPK])
        th_ref[h] = tau - s1
        e1_ref[h] = jnp.exp(s1 - v1[0]) / z
        s2_ref[h] = s2
        e2_ref[h] = jnp.exp(s2 - v2[0])


def _route(h2, wq_hi, wq_lo, k1, k2, tm):
    t, d = h2.shape
    blk = pl.BlockSpec((PEER_HEADS, N_KEYS, tm), lambda i: (0, 0, i))
    shp = jax.ShapeDtypeStruct((PEER_HEADS, N_KEYS, t), F32)
    return pl.pallas_call(
        _route_kernel,
        grid=(t // tm,),
        in_specs=[pl.BlockSpec((tm, d), lambda i: (i, 0)), _full(wq_hi.shape), _full(wq_lo.shape),
                  _full(k1.shape), _full(k2.shape)],
        out_specs=[blk] * 4,
        out_shape=[shp] * 4,
        compiler_params=_params(("arbitrary",)),
        name="route",
    )(h2, wq_hi, wq_lo, k1, k2)


def _gelu(x):
    return 0.5 * x * (1.0 + jnp.tanh(0.7978845608028654 * (x + 0.044715 * (x * x * x))))


def _experts_kernel(h_ref, dn_ref, up_ref, th_ref, e1_ref, s2_ref, e2_ref, x1_ref, mod_ref, o_ref, acc_ref, ga_ref,
                    *, rows):
    e = pl.program_id(1)

    @pl.when(e == 0)
    def _():
        acc_ref[...] = jnp.zeros_like(acc_ref)

    act = _gelu(_dg(dn_ref[...], h_ref[...], NT))
    for ii in range(rows):
        i = e * rows + ii
        gsum = None
        for h in range(PEER_HEADS):
            th = th_ref[h, pl.ds(i, 1), :]
            e1 = e1_ref[h, pl.ds(i, 1), :]
            term = e1 * jnp.where(s2_ref[h] > th, e2_ref[h], 0.0)
            gsum = term if gsum is None else gsum + term
        ga_ref[ii * N_KEYS:(ii + 1) * N_KEYS, :] = (gsum * act[ii * N_KEYS:(ii + 1) * N_KEYS]).astype(BF16)
    acc_ref[...] += _dg(up_ref[...], ga_ref[...])

    @pl.when(e == pl.num_programs(1) - 1)
    def _():
        o_ref[...] = x1_ref[...] + mod_ref[0, 5:6, :] * acc_ref[...].T


def _experts(h2b, down, up_t, th, e1, s2, e2, x1, mod, seq, tm, rows):
    t, d = h2b.shape
    ne = down.shape[0]
    eb = rows * N_KEYS
    per = seq // tm
    rt = pl.BlockSpec((PEER_HEADS, N_KEYS, tm), lambda i, e: (0, 0, i))
    return pl.pallas_call(
        functools.partial(_experts_kernel, rows=rows),
        grid=(t // tm, ne // eb),
        in_specs=[pl.BlockSpec((tm, d), lambda i, e: (i, 0)),
                  pl.BlockSpec((eb, d), lambda i, e: (e, 0)),
                  pl.BlockSpec((d, eb), lambda i, e: (0, e)),
                  rt, rt, rt, rt,
                  pl.BlockSpec((tm, d), lambda i, e: (i, 0)),
                  pl.BlockSpec((1, 8, d), lambda i, e: (i // per, 0, 0))],
        out_specs=pl.BlockSpec((tm, d), lambda i, e: (i, 0)),
        out_shape=jax.ShapeDtypeStruct((t, d), F32),
        scratch_shapes=[pltpu.VMEM((d, tm), F32), pltpu.VMEM((eb, tm), BF16)],
        compiler_params=_params(("arbitrary", "arbitrary")),
        name="experts",
    )(h2b, down, up_t, th, e1, s2, e2, x1, mod)


def _slab_consts():
    i = jnp.arange(LANES)
    seg = jnp.where(i < QK_NOPE, 0, jnp.where(i < QK_NOPE + QK_ROPE, 1, 2))
    same = (seg[:, None] == seg[None, :]) & (seg[:, None] < 2)
    width = jnp.where(seg == 0, QK_NOPE, QK_ROPE).astype(F32)
    ind = jnp.where(same, 1.0 / width[None, :], 0.0).astype(BF16)
    half = QK_ROPE // 2
    src = jnp.where(i < QK_NOPE + half, i + half, i - half)
    sign = jnp.where(i < QK_NOPE + half, -1.0, 1.0)
    rot = jnp.where((seg[None, :] == 1) & (i[:, None] == src[None, :]), sign[None, :], 0.0).astype(BF16)
    inv = ROPE_THETA ** (-jnp.arange(half, dtype=F32) / half)
    freq = jnp.zeros((LANES,), F32).at[QK_NOPE:QK_NOPE + QK_ROPE].set(jnp.concatenate([inv, inv]))
    return ind, rot, freq.reshape(1, LANES)


def _pad_slab(g):
    return jnp.zeros((1, LANES), F32).at[0, :g.shape[0]].set(g)


def kernel(x, c, positions, w_ada, b_ada, norm_mix, w_in, mu_shift, q_a_norm, w_uq, kv_a_norm, w_ukv, q_norm, k_norm,
           w_o_mla, w_decay_up, decay_base, w_aaa_up, aaa_base, w_gate_up, k_k, k_a, r_k, ln_x_w, ln_x_b, w_o_rwkv,
           w_out, norm_ffn, w_query, sub_keys1, sub_keys2, expert_down, expert_up):
    batch, seq, d = x.shape
    t = batch * seq
    depth = w_ada.shape[0]
    tm = min(256, seq)
    chunk = min(128, seq)
    xs = x.reshape(t, d)
    pos = positions.reshape(t, 1)
    ind, rot, freq = _slab_consts()
    lane512 = jnp.arange(RWKV_DIM) // RWKV_HEAD
    seg_sum = (lane512[:, None] == lane512[None, :]).astype(BF16)
    seg_mean = (seg_sum.astype(F32) / RWKV_HEAD).astype(BF16)

    for l in range(depth):
        wi = w_in[l]
        mla_in = Q_LORA + KV_LORA + QK_ROPE
        kr_cols = jnp.zeros((d, LANES), F32).at[:, QK_NOPE:QK_NOPE + QK_ROPE].set(wi[:, Q_LORA + KV_LORA:mla_in])
        w_cat = jnp.concatenate([wi[:, :Q_LORA + KV_LORA], kr_cols, wi[:, mla_in:]], axis=1).astype(BF16)
        wuq = jnp.pad(w_uq[l].reshape(Q_LORA, MLA_HEADS, QK_NOPE + QK_ROPE),
                      ((0, 0), (0, 0), (0, LANES - QK_NOPE - QK_ROPE))).reshape(Q_LORA, MLA_HEADS * LANES).astype(BF16)
        wkv = w_ukv[l].reshape(KV_LORA, MLA_HEADS, QK_NOPE + V_HEAD)
        zeros = jnp.zeros((KV_LORA, MLA_HEADS, LANES - QK_NOPE), F32)
        wuk = jnp.concatenate([wkv[..., :QK_NOPE], zeros], axis=-1).reshape(KV_LORA, MLA_HEADS * LANES).astype(BF16)
        vv = wkv[..., QK_NOPE:]
        odd = (jnp.arange(MLA_HEADS) % 2 == 1)[None, :, None]
        wuv = jnp.concatenate([jnp.where(odd, 0.0, vv), jnp.where(odd, vv, 0.0)], axis=-1)
        wuv = wuv.reshape(KV_LORA, MLA_HEADS * LANES).astype(BF16)
        row = lambda a: a.reshape(1, -1)

        c8 = jnp.pad(c, ((0, 8 - batch % 8 if batch % 8 else 0), (0, 0)))
        mod = _ada(c8, w_ada[l], b_ada[l])[:batch].reshape(batch, N_MOD, d)
        mod = jnp.pad(mod, ((0, 0), (0, 8 - N_MOD), (0, 0)))

        qa, kva, kr, rw, gates = _inproj(xs, mod, norm_mix[l], w_cat, seq, tm)

        mla_consts = (row(q_a_norm[l]), row(kv_a_norm[l]), wuq, wuk, wuv, _pad_slab(q_norm[l]), _pad_slab(k_norm[l]),
                      freq, ind, rot)
        qh, kh, vh = _mlaprep(qa, kva, kr, pos, mla_consts, batch, seq, tm)
        o_mla = _attention(qh, kh, vh, tm)

        rw_consts = (row(mu_shift[l]), w_decay_up[l], row(decay_base[l]), w_aaa_up[l], row(aaa_base[l]),
                     w_gate_up[l], row(k_k[l]), row(k_a[l]), seg_sum)
        r, lw, kk_, v, kkn, bb, g = _rwprep(rw, rw_consts, batch, seq, tm)
        q1, q2, gmat, hmat = _rwchunk(r, lw, kk_, v, kkn, bb, chunk)
        y = _rwscan(q1, q2, gmat, hmat, batch, seq, chunk)

        post_consts = (row(r_k[l]), row(ln_x_w[l]), row(ln_x_b[l]), seg_mean, w_o_mla[l].astype(BF16),
                       w_o_rwkv[l].astype(BF16), w_out[l].astype(BF16), row(norm_ffn[l]))
        x1, h2, h2b = _post(y, r, kk_, v, g, o_mla, gates, xs, mod, post_consts, batch, seq, tm)

        wq_t = w_query[l].T
        wq_hi = wq_t.astype(BF16)
        wq_lo = (wq_t - wq_hi.astype(F32)).astype(BF16)
        th, e1, s2, e2 = _route(h2, wq_hi, wq_lo, sub_keys1[l], sub_keys2[l], tm)
        tme = min(512, seq)
        xs = _experts(h2b, expert_down[l].astype(BF16), expert_up[l].T.astype(BF16), th, e1, s2, e2, x1, mod,
                      seq, tme, 8)
    return xs.reshape(batch, seq, d)
```

```python
import functools

import jax
import jax.numpy as jnp
from jax import lax
from jax.experimental import pallas as pl
from jax.experimental.pallas import tpu as pltpu

F32 = jnp.float32
BF16 = jnp.bfloat16

EPS = 1e-6
GN_EPS = 64e-5
ROPE_THETA = 10000.0
N_MOD = 6
MLA_HEADS = 8
QK_NOPE = 64
QK_ROPE = 32
V_HEAD = 64
Q_LORA = 256
KV_LORA = 128
RWKV_HEADS = 8
RWKV_HEAD = 64
RWKV_DIM = RWKV_HEADS * RWKV_HEAD
DECAY_LORA = 64
AAA_LORA = 64
GATE_LORA = 128
PEER_HEADS = 8
N_KEYS = 128
TOPK = 16

LANES = 128
VMEM_LIMIT = 56 * 1024 * 1024
NEG = -3.0e38

NN = (((1,), (0,)), ((), ()))
NT = (((1,), (1,)), ((), ()))


def _dg(a, b, dims=NN):
    return lax.dot_general(a, b, dims, preferred_element_type=F32)


def _mm(a, b, dims=NN):
    return _dg(a.astype(BF16), b.astype(BF16), dims)


def _split(a):
    hi = a.astype(BF16)
    lo = (a - hi.astype(F32)).astype(BF16)
    return hi, lo


def _mm3(a, b, dims=NN):
    ah, al = _split(a)
    bh, bl = _split(b)
    return _dg(ah, bh, dims) + (_dg(ah, bl, dims) + _dg(al, bh, dims))


def _mm_lx(a, b_exact, dims=NN):
    ah, al = _split(a)
    return _dg(ah, b_exact, dims) + _dg(al, b_exact, dims)


def _params(sem):
    return pltpu.CompilerParams(dimension_semantics=sem, vmem_limit_bytes=VMEM_LIMIT)


def _full(shape):
    nd = len(shape)
    return pl.BlockSpec(shape, lambda *_: (0,) * nd)


def _ada_kernel(c_ref, w_ref, b_ref, o_ref):
    c = c_ref[...]
    s = c * jax.nn.sigmoid(c)
    o_ref[...] = _mm3(s, w_ref[...]) + b_ref[...]


def _ada(c8, w_ada, b_ada):
    d = c8.shape[1]
    n = w_ada.shape[1]
    bn = 1024
    return pl.pallas_call(
        _ada_kernel,
        grid=(n // bn,),
        in_specs=[_full(c8.shape),
                  pl.BlockSpec((d, bn), lambda j: (0, j)),
                  pl.BlockSpec((1, bn), lambda j: (0, j))],
        out_specs=pl.BlockSpec((c8.shape[0], bn), lambda j: (0, j)),
        out_shape=jax.ShapeDtypeStruct((c8.shape[0], n), F32),
        compiler_params=_params(("arbitrary",)),
        name="ada",
    )(c8, w_ada, b_ada.reshape(1, n))


_SEG = (0, Q_LORA, Q_LORA + KV_LORA, Q_LORA + KV_LORA + LANES)
RWKV_IN = 3 * RWKV_DIM + DECAY_LORA + AAA_LORA + GATE_LORA


def _norm_mod(x, g, shift, scale):
    ms = jnp.mean(x * x, axis=-1, keepdims=True)
    h = x * lax.rsqrt(ms + EPS) * g
    return h * (1.0 + scale) + shift


def _inproj_kernel(x_ref, mod_ref, g_ref, w_ref, qa_ref, kva_ref, kr_ref, rw_ref, gate_ref):
    h = _norm_mod(x_ref[...], g_ref[...], mod_ref[0, 0:1, :], mod_ref[0, 1:2, :])
    hb = h.astype(BF16)
    o0, o1, o2, o3 = _SEG
    o4 = o3 + RWKV_IN
    qa_ref[...] = _dg(hb, w_ref[:, o0:o1])
    kva_ref[...] = _dg(hb, w_ref[:, o1:o2])
    kr_ref[...] = _dg(hb, w_ref[:, o2:o3])
    rw_ref[...] = _dg(hb, w_ref[:, o3:o4])
    gate_ref[...] = jax.nn.sigmoid(_dg(hb, w_ref[:, o4:])).astype(BF16)


def _inproj(x2, mod, norm_mix, w_cat, seq, tm):
    t, d = x2.shape
    per = seq // tm
    n_gate = w_cat.shape[1] - _SEG[3] - RWKV_IN
    row = lambda w: pl.BlockSpec((tm, w), lambda i: (i, 0))
    return pl.pallas_call(
        _inproj_kernel,
        grid=(t // tm,),
        in_specs=[row(d),
                  pl.BlockSpec((1, 8, d), lambda i: (i // per, 0, 0)),
                  _full((1, d)),
                  _full(w_cat.shape)],
        out_specs=[row(Q_LORA), row(KV_LORA), row(LANES), row(RWKV_IN), row(n_gate)],
        out_shape=[jax.ShapeDtypeStruct((t, Q_LORA), F32),
                   jax.ShapeDtypeStruct((t, KV_LORA), F32),
                   jax.ShapeDtypeStruct((t, LANES), F32),
                   jax.ShapeDtypeStruct((t, RWKV_IN), F32),
                   jax.ShapeDtypeStruct((t, n_gate), BF16)],
        compiler_params=_params(("arbitrary",)),
        name="inproj",
    )(x2, mod, norm_mix.reshape(1, d), w_cat)


def _mlaprep_kernel(qa_ref, kva_ref, kr_ref, pos_ref, qan_ref, kvan_ref, wuq_ref, wuk_ref, wuv_ref,
                    qg_ref, kg_ref, freq_ref, ind_ref, rot_ref, q_ref, k_ref, v_ref, *, scale):
    ind = ind_ref[...]
    rot = rot_ref[...]
    ang = pos_ref[...].astype(F32) * freq_ref[...]
    cos = jnp.cos(ang)
    sin = jnp.sin(ang)

    def rms(a, g):
        return a * lax.rsqrt(jnp.mean(a * a, axis=-1, keepdims=True) + EPS) * g

    def head_norm(slab, g):
        ms = _mm_lx(slab * slab, ind)
        return slab * lax.rsqrt(ms + EPS) * g

    def rope(a):
        return a * cos + _dg(a.astype(BF16), rot) * sin

    cq = rms(qa_ref[...], qan_ref[...]).astype(BF16)
    ckv = rms(kva_ref[...], kvan_ref[...]).astype(BF16)
    qg = qg_ref[...]
    kg = kg_ref[...]
    k_pe = rope(head_norm(kr_ref[...], kg))
    for h in range(MLA_HEADS):
        sl = slice(h * LANES, (h + 1) * LANES)
        qh = rope(head_norm(_dg(cq, wuq_ref[:, sl]), qg)) * scale
        q_ref[0, h] = qh.astype(BF16)
        kh = head_norm(_dg(ckv, wuk_ref[:, sl]), kg) + k_pe
        k_ref[0, h] = kh.astype(BF16)
        v_ref[0, h] = _dg(ckv, wuv_ref[:, sl]).astype(BF16)


def _mlaprep(qa, kva, kr, pos, consts, batch, seq, tm):
    per = seq // tm
    row = lambda w: pl.BlockSpec((tm, w), lambda b, s: (b * per + s, 0))
    hm = pl.BlockSpec((1, MLA_HEADS, tm, LANES), lambda b, s: (b, 0, s, 0))
    hshape = jax.ShapeDtypeStruct((batch, MLA_HEADS, seq, LANES), BF16)
    return pl.pallas_call(
        functools.partial(_mlaprep_kernel, scale=float((QK_NOPE + QK_ROPE) ** -0.5)),
        grid=(batch, per),
        in_specs=[row(Q_LORA), row(KV_LORA), row(LANES), row(1)] + [_full(c.shape) for c in consts],
        out_specs=[hm, hm, hm],
        out_shape=[hshape, hshape, hshape],
        compiler_params=_params(("arbitrary", "arbitrary")),
        name="mlaprep",
    )(qa, kva, kr, pos, *consts)


def _attn_kernel(q_ref, k_ref, v_ref, o_ref, *, tq):
    qi = pl.program_id(2)
    row = lax.broadcasted_iota(jnp.int32, (tq, tq), 0)
    col = lax.broadcasted_iota(jnp.int32, (tq, tq), 1)
    causal = col <= row

    def block(j, carry, masked):
        new = []
        for hh in range(2):
            m, l, acc = carry[hh]
            k = k_ref[0, hh, pl.ds(pl.multiple_of(j * tq, tq), tq), :]
            v = v_ref[0, hh, pl.ds(pl.multiple_of(j * tq, tq), tq), :]
            s = _dg(q_ref[0, hh], k, NT)
            if masked:
                s = jnp.where(causal, s, NEG)
            m_new = jnp.maximum(m, jnp.max(s, axis=-1, keepdims=True))
            alpha = jnp.exp(m - m_new)
            p = jnp.exp(s - m_new)
            l = alpha * l + jnp.sum(p, axis=-1, keepdims=True)
            acc = alpha * acc + _dg(p.astype(BF16), v)
            new.append((m_new, l, acc))
        return tuple(new)

    one = (jnp.full((tq, 1), NEG, F32), jnp.zeros((tq, 1), F32), jnp.zeros((tq, LANES), F32))
    carry = lax.fori_loop(0, qi, lambda j, c: block(j, c, False), (one, one))
    (_, l0, acc0), (_, l1, acc1) = block(qi, carry, True)
    o_ref[0] = (acc0 / l0 + acc1 / l1).astype(BF16)


def _attention(q, k, v, tq):
    batch, heads, seq, _ = q.shape
    return pl.pallas_call(
        functools.partial(_attn_kernel, tq=tq),
        grid=(batch, heads // 2, seq // tq),
        in_specs=[pl.BlockSpec((1, 2, tq, LANES), lambda b, p, i: (b, p, i, 0)),
                  pl.BlockSpec((1, 2, seq, LANES), lambda b, p, i: (b, p, 0, 0)),
                  pl.BlockSpec((1, 2, seq, LANES), lambda b, p, i: (b, p, 0, 0))],
        out_specs=pl.BlockSpec((1, tq, LANES), lambda b, p, i: (b, i, p)),
        out_shape=jax.ShapeDtypeStruct((batch, seq, heads // 2 * LANES), BF16),
        compiler_params=_params(("arbitrary", "arbitrary", "arbitrary")),
        name="attn",
    )(q, k, v)


def _softplus(z):
    return jnp.maximum(z, 0.0) + jnp.log(1.0 + jnp.exp(-jnp.abs(z)))


def _rwprep_kernel(p_ref, prev_ref, mu_ref, wd_ref, db_ref, wa_ref, ab_ref, wg_ref, kk_ref, ka_ref, seg_ref,
                   r_ref, lw_ref, k_ref, v_ref, kkn_ref, b_ref, g_ref):
    p = p_ref[...]
    tm = p.shape[0]
    first = pl.program_id(1) == 0
    last_prev = jnp.where(first, 0.0, prev_ref[7:8, :])
    rows = lax.broadcasted_iota(jnp.int32, p.shape, 0)
    prev = jnp.where(rows == 0, last_prev, pltpu.roll(p, 1, 0))
    p = p + (prev - p) * mu_ref[...]
    d = RWKV_DIM
    r = p[:, 0:d]
    k = p[:, d:2 * d]
    v = p[:, 2 * d:3 * d]
    o = 3 * d
    wd = p[:, o:o + DECAY_LORA]
    ad = p[:, o + DECAY_LORA:o + DECAY_LORA + AAA_LORA]
    gd = p[:, o + DECAY_LORA + AAA_LORA:]
    w = -_softplus(-(db_ref[...] + _mm3(jnp.tanh(wd), wd_ref[...]))) - 0.5
    a = jax.nn.sigmoid(ab_ref[...] + _mm3(ad, wa_ref[...]))
    kk = k * kk_ref[...]
    n2 = _mm_lx(kk * kk, seg_ref[...])
    kk = kk / jnp.maximum(jnp.sqrt(n2), 1e-12)
    r_ref[...] = r
    lw_ref[...] = -jnp.exp(w)
    k_ref[...] = k * (1.0 + (a - 1.0) * ka_ref[...])
    v_ref[...] = v
    kkn_ref[...] = kk
    b_ref[...] = kk * a
    g_ref[...] = _mm(jax.nn.sigmoid(gd), wg_ref[...])
    del tm


def _rwprep(rw, consts, batch, seq, tm):
    t = rw.shape[0]
    per = seq // tm
    row = pl.BlockSpec((tm, RWKV_DIM), lambda b, s: (b * per + s, 0))
    shp = jax.ShapeDtypeStruct((t, RWKV_DIM), F32)
    return pl.pallas_call(
        _rwprep_kernel,
        grid=(batch, per),
        in_specs=[pl.BlockSpec((tm, RWKV_IN), lambda b, s: (b * per + s, 0)),
                  pl.BlockSpec((8, RWKV_IN), lambda b, s: (jnp.maximum((b * per + s) * (tm // 8) - 1, 0), 0))]
                 + [_full(c.shape) for c in consts],
        out_specs=[row] * 7,
        out_shape=[shp] * 7,
        compiler_params=_params(("arbitrary", "arbitrary")),
        name="rwprep",
    )(rw, rw, *consts)


def _chunk_math(r, lw, k, v, kk, b, mm):
    c = r.shape[0]
    row = lax.broadcasted_iota(jnp.int32, (c, c), 0)
    col = lax.broadcasted_iota(jnp.int32, (c, c), 1)
    incl = row >= col
    strict = row > col
    tri = incl.astype(BF16)
    l_hi = lw.astype(BF16)
    l_r1 = lw - l_hi.astype(F32)
    l_mid = l_r1.astype(BF16)
    l_lo = (l_r1 - l_mid.astype(F32)).astype(BF16)
    cum = _dg(tri, l_hi) + (_dg(tri, l_mid) + _dg(tri, l_lo))
    tot = cum[c - 1:c, :]
    e_pos = jnp.exp(cum)
    e_neg = jnp.exp(-cum)
    e_rem = jnp.exp(tot - cum)
    kk_t = kk * jnp.exp(cum - lw)
    b_t = b * e_neg
    k_t = k * e_neg
    r_t = r * e_pos
    b_h = b * e_rem
    k_h = k * e_rem
    lane = lax.broadcasted_iota(jnp.int32, (1, LANES), 1)
    masks = [(lane < 64).astype(F32), (lane >= 64).astype(F32)]
    n_pairs = r.shape[1] // LANES
    heads = [(p, hh) for p in range(n_pairs) for hh in range(2)]
    nh = len(heads)
    sl = lambda a, p: a[:, p * LANES:(p + 1) * LANES]

    kk_m = [sl(kk_t, p) * masks[hh] for p, hh in heads]
    r_m = [sl(r_t, p) * masks[hh] for p, hh in heads]
    rhs = [jnp.concatenate([sl(b_t, p), sl(k_t, p)], axis=0) for p in range(n_pairs)]
    big = [mm(jnp.concatenate([kk_m[i], r_m[i]], axis=0), rhs[p], NT) for i, (p, _) in enumerate(heads)]
    a_ab = [jnp.where(strict, g_[:c, :c], 0.0) for g_ in big]
    b_rb = [jnp.where(incl, g_[c:, :c], 0.0) for g_ in big]
    av = [mm(jnp.concatenate([jnp.where(strict, big[i][:c, c:], 0.0), jnp.where(incl, big[i][c:, c:], 0.0)], axis=0),
             sl(v, p)) for i, (p, _) in enumerate(heads)]
    z = [jnp.concatenate([kk_m[i], av[i][:c]], axis=1) for i in range(nh)]
    n = a_ab
    z = [z[i] - mm(n[i], z[i]) for i in range(nh)]
    for _ in range(c.bit_length() - 2):
        n = [mm(a, a) for a in n]
        z = [z[i] + mm(n[i], z[i]) for i in range(nh)]
    p12 = [jnp.concatenate([z[i][:, :LANES], z[i][:, LANES:] * masks[hh]], axis=1) for i, (_, hh) in enumerate(heads)]
    bp = [mm(b_rb[i], p12[i]) for i in range(nh)]
    q1h = [r_m[i] - bp[i][:, :LANES] for i in range(nh)]
    q2h = [(av[i][c:] - bp[i][:, LANES:]) * masks[hh] for i, (_, hh) in enumerate(heads)]
    q1 = jnp.concatenate([q1h[2 * p] + q1h[2 * p + 1] for p in range(n_pairs)], axis=1)
    q2 = jnp.concatenate([q2h[2 * p] + q2h[2 * p + 1] for p in range(n_pairs)], axis=1)
    r128 = lax.broadcasted_iota(jnp.int32, (LANES, LANES), 0)
    c128 = lax.broadcasted_iota(jnp.int32, (LANES, LANES), 1)
    same = (r128 >= 64) == (c128 >= 64)
    btp = [mm(sl(b_h, p).T, p12[2 * p] + p12[2 * p + 1]) for p in range(n_pairs)]
    ktv = [mm(sl(k_h, p).T, sl(v, p)) for p in range(n_pairs)]
    e_tot = jnp.exp(tot)
    g = [jnp.where(r128 == c128, sl(e_tot, p), 0.0) - jnp.where(same, btp[p][:, :LANES], 0.0) for p in range(n_pairs)]
    hmat = [jnp.where(same, ktv[p] - btp[p][:, LANES:], 0.0) for p in range(n_pairs)]
    return q1, q2, g, hmat


def _rwchunk_kernel(r_ref, lw_ref, k_ref, v_ref, kk_ref, b_ref, q1_ref, q2_ref, g_ref, h_ref):
    q1, q2, g, hmat = _chunk_math(r_ref[...], lw_ref[...], k_ref[...], v_ref[...], kk_ref[...], b_ref[...], _mm)
    q1_ref[...] = q1
    q2_ref[...] = q2
    for p in range(len(g)):
        g_ref[0, p] = g[p]
        h_ref[0, p] = hmat[p]


def _rwchunk(r, lw, k, v, kk, b, c):
    t = r.shape[0]
    pairs = RWKV_DIM // LANES
    blk = pl.BlockSpec((c, RWKV_DIM), lambda i: (i, 0))
    sq = pl.BlockSpec((1, pairs, LANES, LANES), lambda i: (i, 0, 0, 0))
    return pl.pallas_call(
        _rwchunk_kernel,
        grid=(t // c,),
        in_specs=[blk] * 6,
        out_specs=[blk, blk, sq, sq],
        out_shape=[jax.ShapeDtypeStruct((t, RWKV_DIM), F32)] * 2
                  + [jax.ShapeDtypeStruct((t // c, pairs, LANES, LANES), F32)] * 2,
        compiler_params=_params(("arbitrary",)),
        name="rwchunk",
    )(r, lw, k, v, kk, b)


def _rwscan_kernel(q1_ref, q2_ref, g_ref, h_ref, y_ref, s_ref):
    @pl.when(pl.program_id(1) == 0)
    def _():
        s_ref[...] = jnp.zeros_like(s_ref)

    for p in range(RWKV_DIM // LANES):
        sl = slice(p * LANES, (p + 1) * LANES)
        s = s_ref[p]
        y_ref[:, sl] = _mm3(q1_ref[:, sl], s) + q2_ref[:, sl]
        s_ref[p] = _mm3(g_ref[0, p], s) + h_ref[0, p]


def _rwscan(q1, q2, g, hmat, batch, seq, c):
    t = q1.shape[0]
    per = seq // c
    pairs = RWKV_DIM // LANES
    blk = pl.BlockSpec((c, RWKV_DIM), lambda b, i: (b * per + i, 0))
    sq = pl.BlockSpec((1, pairs, LANES, LANES), lambda b, i: (b * per + i, 0, 0, 0))
    return pl.pallas_call(
        _rwscan_kernel,
        grid=(batch, per),
        in_specs=[blk, blk, sq, sq],
        out_specs=blk,
        out_shape=jax.ShapeDtypeStruct((t, RWKV_DIM), F32),
        scratch_shapes=[pltpu.VMEM((pairs, LANES, LANES), F32)],
        compiler_params=_params(("arbitrary", "arbitrary")),
        name="rwscan",
    )(q1, q2, g, hmat)


def _post_kernel(y_ref, r_ref, k_ref, v_ref, g_ref, om_ref, gate_ref, x_ref, mod_ref, rk_ref, lnw_ref, lnb_ref,
                 segm_ref, wom_ref, wor_ref, wout_ref, nf_ref, x1_ref, h2_ref, h2b_ref):
    segm = segm_ref[...]
    y = y_ref[...]
    mu = _mm_lx(y, segm)
    dv = y - mu
    var = _mm_lx(dv * dv, segm)
    yn = dv * lax.rsqrt(var + GN_EPS) * lnw_ref[...] + lnb_ref[...]
    bonus = _mm_lx(r_ref[...] * k_ref[...] * rk_ref[...], segm) * float(RWKV_HEAD)
    yy = (yn + bonus * v_ref[...]) * g_ref[...]
    o_r = _mm(yy, wor_ref[...])
    o_m = _dg(om_ref[0], wom_ref[...])
    d = o_m.shape[1]
    gm = gate_ref[:, 0:d].astype(F32)
    gr = gate_ref[:, d:2 * d].astype(F32)
    mix = _mm(gm * o_m + gr * o_r, wout_ref[...])
    x1 = x_ref[...] + mod_ref[0, 2:3, :] * mix
    x1_ref[...] = x1
    h2 = _norm_mod(x1, nf_ref[...], mod_ref[0, 3:4, :], mod_ref[0, 4:5, :])
    h2_ref[...] = h2
    h2b_ref[...] = h2.astype(BF16)


def _post(y, r, k, v, g, o_mla, gates, x2, mod, consts, batch, seq, tm):
    t, d = x2.shape
    per = seq // tm
    row = lambda w: pl.BlockSpec((tm, w), lambda b, s: (b * per + s, 0))
    return pl.pallas_call(
        _post_kernel,
        grid=(batch, per),
        in_specs=[row(RWKV_DIM)] * 5
                 + [pl.BlockSpec((1, tm, o_mla.shape[2]), lambda b, s: (b, s, 0)),
                    row(gates.shape[1]), row(d),
                    pl.BlockSpec((1, 8, d), lambda b, s: (b, 0, 0))]
                 + [_full(c.shape) for c in consts],
        out_specs=[row(d), row(d), row(d)],
        out_shape=[jax.ShapeDtypeStruct((t, d), F32), jax.ShapeDtypeStruct((t, d), F32),
                   jax.ShapeDtypeStruct((t, d), BF16)],
        compiler_params=_params(("arbitrary", "arbitrary")),
        name="post",
    )(y, r, k, v, g, o_mla, gates, x2, mod, *consts)


def _cand_pairs(n):
    return [(a, b) for a in range(n) for b in range(n) if (a + 1) * (b + 1) <= n]


def _top_values(s, n, with_rank=False):
    vals = []
    cur = s
    rank = jnp.full(s.shape, float(n), F32) if with_rank else None
    for i in range(n):
        m = jnp.max(cur, axis=0, keepdims=True)
        vals.append(m)
        hit = cur >= m
        if with_rank:
            rank = jnp.where(hit, float(i), rank)
        if i + 1 < n:
            cur = jnp.where(hit, NEG, cur)
    return vals, rank


def _route_kernel(h_ref, wqh_ref, wql_ref, k1_ref, k2_ref, cnt_ref, e1_ref, rk_ref, e2_ref):
    hh, hl = _split(h_ref[...])
    q = _dg(wqh_ref[...], hh, NT) + (_dg(wqh_ref[...], hl, NT) + _dg(wql_ref[...], hh, NT))
    k1 = k1_ref[...]
    k2 = k2_ref[...]
    nk = TOPK + 1
    pairs = _cand_pairs(nk)
    for h in range(PEER_HEADS):
        base = 2 * N_KEYS * h
        s1 = _mm3(k1, q[base:base + N_KEYS])
        s2 = _mm3(k2, q[base + N_KEYS:base + 2 * N_KEYS])
        v1, _ = _top_values(s1, nk)
        v2, rank2 = _top_values(s2, nk, with_rank=True)
        pad = [jnp.full_like(v1[0], NEG)] * (-len(pairs) % 8)
        cand = jnp.concatenate([v1[a] + v2[b] for a, b in pairs] + pad, axis=0)
        top, _ = _top_values(cand, nk)
        z = sum(jnp.exp(top[i] - top[0]) for i in range(1, TOPK)) + 1.0
        tau = 0.5 * (top[TOPK - 1] + top[TOPK])
        theta = tau - s1
        cnt_ref[h] = sum((v2[b] > theta).astype(F32) for b in range(TOPK))
        e1_ref[h] = 0.5 * jnp.exp(s1 - v1[0]) / z
        rk_ref[h] = rank2.astype(BF16)
        e2_ref[h] = jnp.exp(s2 - v2[0]).astype(BF16)


def _route(h2, wq_hi, wq_lo, k1, k2, tm):
    t, d = h2.shape
    blk = pl.BlockSpec((PEER_HEADS, N_KEYS, tm), lambda i: (0, 0, i))
    wide = jax.ShapeDtypeStruct((PEER_HEADS, N_KEYS, t), F32)
    half = jax.ShapeDtypeStruct((PEER_HEADS, N_KEYS, t), BF16)
    return pl.pallas_call(
        _route_kernel,
        grid=(t // tm,),
        in_specs=[pl.BlockSpec((tm, d), lambda i: (i, 0)), _full(wq_hi.shape), _full(wq_lo.shape),
                  _full(k1.shape), _full(k2.shape)],
        out_specs=[blk] * 4,
        out_shape=[wide, wide, half, half],
        compiler_params=_params(("arbitrary",)),
        name="route",
    )(h2, wq_hi, wq_lo, k1, k2)


def _gelu2(x):
    return x * (1.0 + jnp.tanh(x * (0.7978845608028654 + 0.035677408136300125 * (x * x))))


def _bf16_rows(row):
    tile = jnp.broadcast_to(row, (16, row.shape[1])).astype(BF16)
    return jnp.concatenate([tile] * (N_KEYS // 16), axis=0)


def _experts_kernel(h_ref, dn_ref, up_ref, cnt_ref, e1_ref, rk_ref, e2_ref, x1_ref, mod_ref, o_ref, acc_ref, ga_ref,
                    *, rows, sub):
    e = pl.program_id(1)
    nb = pl.num_programs(1) - 1
    zero = jnp.zeros((), BF16)
    width = sub * N_KEYS

    @pl.when(e == 0)
    def _():
        acc_ref[...] = jnp.zeros_like(acc_ref)
        ga_ref[1] = jnp.zeros(ga_ref.shape[1:], BF16)

    nsub = rows // sub
    d = acc_ref.shape[0]

    def project_previous(part, parts):
        rs = slice(part * (d // parts), (part + 1) * (d // parts))
        acc_ref[rs, :] += _dg(up_ref[rs, :], ga_ref[(e + 1) % 2])

    @pl.when(e < nb)
    def _():
        raw = {}
        for s in range(nsub + 1):
            if s < nsub:
                raw[s] = _dg(dn_ref[s * width:(s + 1) * width, :], h_ref[...], NT)
            if s >= 1:
                project_previous(s - 1, nsub)
                act = _gelu2(raw.pop(s - 1))
                for ii in range(sub):
                    r0 = (s - 1) * sub + ii
                    i = e * rows + r0
                    gsum = None
                    for h in range(PEER_HEADS):
                        cnt = _bf16_rows(cnt_ref[h, pl.ds(i, 1), :])
                        e1 = _bf16_rows(e1_ref[h, pl.ds(i, 1), :])
                        term = e1 * jnp.where(rk_ref[h] < cnt, e2_ref[h], zero)
                        gsum = term if gsum is None else gsum + term
                    ga_ref[e % 2, r0 * N_KEYS:(r0 + 1) * N_KEYS, :] = (
                        gsum * act[ii * N_KEYS:(ii + 1) * N_KEYS].astype(BF16))

    @pl.when(e == nb)
    def _():
        project_previous(0, 1)
        o_ref[...] = x1_ref[...] + mod_ref[0, 5:6, :] * acc_ref[...].T


def _experts(h2b, down, up_t, cnt, e1, rk, e2, x1, mod, seq, tm, rows, sub):
    t, d = h2b.shape
    ne = down.shape[0]
    eb = rows * N_KEYS
    nb = ne // eb
    per = seq // tm
    rt = pl.BlockSpec((PEER_HEADS, N_KEYS, tm), lambda i, e: (0, 0, i))
    return pl.pallas_call(
        functools.partial(_experts_kernel, rows=rows, sub=sub),
        grid=(t // tm, nb + 1),
        in_specs=[pl.BlockSpec((tm, d), lambda i, e: (i, 0)),
                  pl.BlockSpec((eb, d), lambda i, e: (jnp.minimum(e, nb - 1), 0)),
                  pl.BlockSpec((d, eb), lambda i, e: (0, jnp.maximum(e - 1, 0))),
                  rt, rt, rt, rt,
                  pl.BlockSpec((tm, d), lambda i, e: (i, 0)),
                  pl.BlockSpec((1, 8, d), lambda i, e: (i // per, 0, 0))],
        out_specs=pl.BlockSpec((tm, d), lambda i, e: (i, 0)),
        out_shape=jax.ShapeDtypeStruct((t, d), F32),
        scratch_shapes=[pltpu.VMEM((d, tm), F32), pltpu.VMEM((2, eb, tm), BF16)],
        compiler_params=_params(("arbitrary", "arbitrary")),
        name="experts",
    )(h2b, down, up_t, cnt, e1, rk, e2, x1, mod)


def _slab_consts():
    i = jnp.arange(LANES)
    seg = jnp.where(i < QK_NOPE, 0, jnp.where(i < QK_NOPE + QK_ROPE, 1, 2))
    same = (seg[:, None] == seg[None, :]) & (seg[:, None] < 2)
    width = jnp.where(seg == 0, QK_NOPE, QK_ROPE).astype(F32)
    ind = jnp.where(same, 1.0 / width[None, :], 0.0).astype(BF16)
    half = QK_ROPE // 2
    src = jnp.where(i < QK_NOPE + half, i + half, i - half)
    sign = jnp.where(i < QK_NOPE + half, -1.0, 1.0)
    rot = jnp.where((seg[None, :] == 1) & (i[:, None] == src[None, :]), sign[None, :], 0.0).astype(BF16)
    inv = ROPE_THETA ** (-jnp.arange(half, dtype=F32) / half)
    freq = jnp.zeros((LANES,), F32).at[QK_NOPE:QK_NOPE + QK_ROPE].set(jnp.concatenate([inv, inv]))
    return ind, rot, freq.reshape(1, LANES)


def _pad_slab(g):
    return jnp.zeros((1, LANES), F32).at[0, :g.shape[0]].set(g)


def kernel(x, c, positions, w_ada, b_ada, norm_mix, w_in, mu_shift, q_a_norm, w_uq, kv_a_norm, w_ukv, q_norm, k_norm,
           w_o_mla, w_decay_up, decay_base, w_aaa_up, aaa_base, w_gate_up, k_k, k_a, r_k, ln_x_w, ln_x_b, w_o_rwkv,
           w_out, norm_ffn, w_query, sub_keys1, sub_keys2, expert_down, expert_up):
    batch, seq, d = x.shape
    t = batch * seq
    depth = w_ada.shape[0]
    tm = min(256, seq)
    chunk = min(128, seq)
    xs = x.reshape(t, d)
    pos = positions.reshape(t, 1)
    ind, rot, freq = _slab_consts()
    lane512 = jnp.arange(RWKV_DIM) // RWKV_HEAD
    seg_sum = (lane512[:, None] == lane512[None, :]).astype(BF16)
    seg_mean = (seg_sum.astype(F32) / RWKV_HEAD).astype(BF16)

    for l in range(depth):
        wi = w_in[l]
        mla_in = Q_LORA + KV_LORA + QK_ROPE
        kr_cols = jnp.zeros((d, LANES), F32).at[:, QK_NOPE:QK_NOPE + QK_ROPE].set(wi[:, Q_LORA + KV_LORA:mla_in])
        w_cat = jnp.concatenate([wi[:, :Q_LORA + KV_LORA], kr_cols, wi[:, mla_in:]], axis=1).astype(BF16)
        wuq = jnp.pad(w_uq[l].reshape(Q_LORA, MLA_HEADS, QK_NOPE + QK_ROPE),
                      ((0, 0), (0, 0), (0, LANES - QK_NOPE - QK_ROPE))).reshape(Q_LORA, MLA_HEADS * LANES).astype(BF16)
        wkv = w_ukv[l].reshape(KV_LORA, MLA_HEADS, QK_NOPE + V_HEAD)
        zeros = jnp.zeros((KV_LORA, MLA_HEADS, LANES - QK_NOPE), F32)
        wuk = jnp.concatenate([wkv[..., :QK_NOPE], zeros], axis=-1).reshape(KV_LORA, MLA_HEADS * LANES).astype(BF16)
        vv = wkv[..., QK_NOPE:]
        odd = (jnp.arange(MLA_HEADS) % 2 == 1)[None, :, None]
        wuv = jnp.concatenate([jnp.where(odd, 0.0, vv), jnp.where(odd, vv, 0.0)], axis=-1)
        wuv = wuv.reshape(KV_LORA, MLA_HEADS * LANES).astype(BF16)
        row = lambda a: a.reshape(1, -1)

        c8 = jnp.pad(c, ((0, 8 - batch % 8 if batch % 8 else 0), (0, 0)))
        mod = _ada(c8, w_ada[l], b_ada[l])[:batch].reshape(batch, N_MOD, d)
        mod = jnp.pad(mod, ((0, 0), (0, 8 - N_MOD), (0, 0)))

        qa, kva, kr, rw, gates = _inproj(xs, mod, norm_mix[l], w_cat, seq, tm)

        mla_consts = (row(q_a_norm[l]), row(kv_a_norm[l]), wuq, wuk, wuv, _pad_slab(q_norm[l]), _pad_slab(k_norm[l]),
                      freq, ind, rot)
        qh, kh, vh = _mlaprep(qa, kva, kr, pos, mla_consts, batch, seq, tm)
        o_mla = _attention(qh, kh, vh, min(512, seq))

        rw_consts = (row(mu_shift[l]), w_decay_up[l], row(decay_base[l]), w_aaa_up[l], row(aaa_base[l]),
                     w_gate_up[l], row(k_k[l]), row(k_a[l]), seg_sum)
        r, lw, kk_, v, kkn, bb, g = _rwprep(rw, rw_consts, batch, seq, tm)
        q1, q2, gmat, hmat = _rwchunk(r, lw, kk_, v, kkn, bb, chunk)
        y = _rwscan(q1, q2, gmat, hmat, batch, seq, chunk)

        post_consts = (row(r_k[l]), row(ln_x_w[l]), row(ln_x_b[l]), seg_mean, w_o_mla[l].astype(BF16),
                       w_o_rwkv[l].astype(BF16), w_out[l].astype(BF16), row(norm_ffn[l]))
        x1, h2, h2b = _post(y, r, kk_, v, g, o_mla, gates, xs, mod, post_consts, batch, seq, tm)

        wq_t = w_query[l].T
        wq_hi = wq_t.astype(BF16)
        wq_lo = (wq_t - wq_hi.astype(F32)).astype(BF16)
        cnt, e1, rk, e2 = _route(h2, wq_hi, wq_lo, sub_keys1[l], sub_keys2[l], tm)
        tme = min(512, seq)
        xs = _experts(h2b, expert_down[l].astype(BF16), expert_up[l].T.astype(BF16), cnt, e1, rk, e2, x1, mod,
                      seq, tme, 8, 2)
    return xs.reshape(batch, seq, d)
```

```python
import functools

import jax
import jax.numpy as jnp
from jax import lax
from jax.experimental import pallas as pl
from jax.experimental.pallas import tpu as pltpu

F32 = jnp.float32
BF16 = jnp.bfloat16

EPS = 1e-6
GN_EPS = 64e-5
ROPE_THETA = 10000.0
N_MOD = 6
MLA_HEADS = 8
QK_NOPE = 64
QK_ROPE = 32
V_HEAD = 64
Q_LORA = 256
KV_LORA = 128
RWKV_HEADS = 8
RWKV_HEAD = 64
RWKV_DIM = RWKV_HEADS * RWKV_HEAD
DECAY_LORA = 64
AAA_LORA = 64
GATE_LORA = 128
PEER_HEADS = 8
N_KEYS = 128
TOPK = 16

LANES = 128
VMEM_LIMIT = 56 * 1024 * 1024
NEG = -3.0e38

_ONES_LANE = (V_HEAD, 0)

NN = (((1,), (0,)), ((), ()))
NT = (((1,), (1,)), ((), ()))


def _dg(a, b, dims=NN):
    return lax.dot_general(a, b, dims, preferred_element_type=F32)


def _mm(a, b, dims=NN):
    return _dg(a.astype(BF16), b.astype(BF16), dims)


def _split(a):
    hi = a.astype(BF16)
    lo = (a - hi.astype(F32)).astype(BF16)
    return hi, lo


def _mm3(a, b, dims=NN):
    ah, al = _split(a)
    bh, bl = _split(b)
    return _dg(ah, bh, dims) + (_dg(ah, bl, dims) + _dg(al, bh, dims))


def _mm_lx(a, b_exact, dims=NN):
    ah, al = _split(a)
    return _dg(ah, b_exact, dims) + _dg(al, b_exact, dims)


def _params(sem):
    return pltpu.CompilerParams(dimension_semantics=sem, vmem_limit_bytes=VMEM_LIMIT)


def _full(shape):
    nd = len(shape)
    return pl.BlockSpec(shape, lambda *_: (0,) * nd)


def _ada_kernel(c_ref, w_ref, b_ref, o_ref):
    c = c_ref[...]
    s = c * jax.nn.sigmoid(c)
    o_ref[...] = _mm3(s, w_ref[...]) + b_ref[...]


def _ada(c8, w_ada, b_ada):
    d = c8.shape[1]
    n = w_ada.shape[1]
    bn = 1024
    return pl.pallas_call(
        _ada_kernel,
        grid=(n // bn,),
        in_specs=[_full(c8.shape),
                  pl.BlockSpec((d, bn), lambda j: (0, j)),
                  pl.BlockSpec((1, bn), lambda j: (0, j))],
        out_specs=pl.BlockSpec((c8.shape[0], bn), lambda j: (0, j)),
        out_shape=jax.ShapeDtypeStruct((c8.shape[0], n), F32),
        compiler_params=_params(("arbitrary",)),
        name="ada",
    )(c8, w_ada, b_ada.reshape(1, n))


_SEG = (0, Q_LORA, Q_LORA + KV_LORA, Q_LORA + KV_LORA + LANES)
RWKV_IN = 3 * RWKV_DIM + DECAY_LORA + AAA_LORA + GATE_LORA


def _norm_mod(x, g, shift, scale):
    ms = jnp.mean(x * x, axis=-1, keepdims=True)
    h = x * lax.rsqrt(ms + EPS) * g
    return h * (1.0 + scale) + shift


def _inproj_kernel(x_ref, mod_ref, g_ref, w_ref, qa_ref, kva_ref, kr_ref, rw_ref, gate_ref):
    h = _norm_mod(x_ref[...], g_ref[...], mod_ref[0, 0:1, :], mod_ref[0, 1:2, :])
    hb = h.astype(BF16)
    o0, o1, o2, o3 = _SEG
    o4 = o3 + RWKV_IN
    qa_ref[...] = _dg(hb, w_ref[:, o0:o1])
    kva_ref[...] = _dg(hb, w_ref[:, o1:o2])
    kr_ref[...] = _dg(hb, w_ref[:, o2:o3])
    rw_ref[...] = _dg(hb, w_ref[:, o3:o4])
    gate_ref[...] = jax.nn.sigmoid(_dg(hb, w_ref[:, o4:])).astype(BF16)


def _inproj(x2, mod, norm_mix, w_cat, seq, tm):
    t, d = x2.shape
    per = seq // tm
    n_gate = w_cat.shape[1] - _SEG[3] - RWKV_IN
    row = lambda w: pl.BlockSpec((tm, w), lambda i: (i, 0))
    return pl.pallas_call(
        _inproj_kernel,
        grid=(t // tm,),
        in_specs=[row(d),
                  pl.BlockSpec((1, 8, d), lambda i: (i // per, 0, 0)),
                  _full((1, d)),
                  _full(w_cat.shape)],
        out_specs=[row(Q_LORA), row(KV_LORA), row(LANES), row(RWKV_IN), row(n_gate)],
        out_shape=[jax.ShapeDtypeStruct((t, Q_LORA), F32),
                   jax.ShapeDtypeStruct((t, KV_LORA), F32),
                   jax.ShapeDtypeStruct((t, LANES), F32),
                   jax.ShapeDtypeStruct((t, RWKV_IN), F32),
                   jax.ShapeDtypeStruct((t, n_gate), BF16)],
        compiler_params=_params(("arbitrary",)),
        name="inproj",
    )(x2, mod, norm_mix.reshape(1, d), w_cat)


def _mlaprep_kernel(qa_ref, kva_ref, kr_ref, pos_ref, qan_ref, kvan_ref, wuq_ref, wuk_ref, wuv_ref,
                    qg_ref, kg_ref, freq_ref, ind_ref, rot_ref, q_ref, k_ref, v_ref, *, scale):
    ind = ind_ref[...]
    rot = rot_ref[...]
    ang = pos_ref[...].astype(F32) * freq_ref[...]
    cos = jnp.cos(ang)
    sin = jnp.sin(ang)

    def rms(a, g):
        return a * lax.rsqrt(jnp.mean(a * a, axis=-1, keepdims=True) + EPS) * g

    def head_norm(slab, g):
        ms = _mm_lx(slab * slab, ind)
        return slab * lax.rsqrt(ms + EPS) * g

    def rope(a):
        return a * cos + _dg(a.astype(BF16), rot) * sin

    cq = rms(qa_ref[...], qan_ref[...]).astype(BF16)
    ckv = rms(kva_ref[...], kvan_ref[...]).astype(BF16)
    qg = qg_ref[...]
    kg = kg_ref[...]
    k_pe = rope(head_norm(kr_ref[...], kg))
    for h in range(MLA_HEADS):
        sl = slice(h * LANES, (h + 1) * LANES)
        qh = rope(head_norm(_dg(cq, wuq_ref[:, sl]), qg)) * scale
        q_ref[0, h] = qh.astype(BF16)
        kh = head_norm(_dg(ckv, wuk_ref[:, sl]), kg) + k_pe
        k_ref[0, h] = kh.astype(BF16)
        ones = (lax.broadcasted_iota(jnp.int32, (1, LANES), 1) == _ONES_LANE[h % 2]).astype(F32)
        v_ref[0, h] = (_dg(ckv, wuv_ref[:, sl]) + ones).astype(BF16)


def _mlaprep(qa, kva, kr, pos, consts, batch, seq, tm):
    per = seq // tm
    row = lambda w: pl.BlockSpec((tm, w), lambda b, s: (b * per + s, 0))
    hm = pl.BlockSpec((1, MLA_HEADS, tm, LANES), lambda b, s: (b, 0, s, 0))
    hshape = jax.ShapeDtypeStruct((batch, MLA_HEADS, seq, LANES), BF16)
    return pl.pallas_call(
        functools.partial(_mlaprep_kernel, scale=float((QK_NOPE + QK_ROPE) ** -0.5)),
        grid=(batch, per),
        in_specs=[row(Q_LORA), row(KV_LORA), row(LANES), row(1)] + [_full(c.shape) for c in consts],
        out_specs=[hm, hm, hm],
        out_shape=[hshape, hshape, hshape],
        compiler_params=_params(("arbitrary", "arbitrary")),
        name="mlaprep",
    )(qa, kva, kr, pos, *consts)


def _attn_kernel(q_ref, k_ref, v_ref, o_ref, *, tq):
    qi = pl.program_id(2)
    row = lax.broadcasted_iota(jnp.int32, (tq, tq), 0)
    col = lax.broadcasted_iota(jnp.int32, (tq, tq), 1)
    causal = col <= row

    def block(j, carry, masked):
        new = []
        for hh in range(2):
            m, acc = carry[hh]
            k = k_ref[0, hh, pl.ds(pl.multiple_of(j * tq, tq), tq), :]
            v = v_ref[0, hh, pl.ds(pl.multiple_of(j * tq, tq), tq), :]
            s = _dg(q_ref[0, hh], k, NT)
            if masked:
                s = jnp.where(causal, s, NEG)
            m_new = jnp.maximum(m, jnp.max(s, axis=-1, keepdims=True))
            p = jnp.exp((s - m_new).astype(BF16))
            acc = jnp.exp(m - m_new) * acc + _dg(p, v)
            new.append((m_new, acc))
        return tuple(new)

    one = (jnp.full((tq, 1), NEG, F32), jnp.zeros((tq, LANES), F32))
    carry = lax.fori_loop(0, qi, lambda j, c: block(j, c, False), (one, one))
    (_, acc0), (_, acc1) = block(qi, carry, True)
    l0 = acc0[:, _ONES_LANE[0]:_ONES_LANE[0] + 1]
    l1 = acc1[:, _ONES_LANE[1]:_ONES_LANE[1] + 1]
    lane = lax.broadcasted_iota(jnp.int32, (1, LANES), 1)
    o_ref[0] = jnp.where(lane < V_HEAD, acc0 / l0, acc1 / l1).astype(BF16)


def _attention(q, k, v, tq):
    batch, heads, seq, _ = q.shape
    return pl.pallas_call(
        functools.partial(_attn_kernel, tq=tq),
        grid=(batch, heads // 2, seq // tq),
        in_specs=[pl.BlockSpec((1, 2, tq, LANES), lambda b, p, i: (b, p, i, 0)),
                  pl.BlockSpec((1, 2, seq, LANES), lambda b, p, i: (b, p, 0, 0)),
                  pl.BlockSpec((1, 2, seq, LANES), lambda b, p, i: (b, p, 0, 0))],
        out_specs=pl.BlockSpec((1, tq, LANES), lambda b, p, i: (b, i, p)),
        out_shape=jax.ShapeDtypeStruct((batch, seq, heads // 2 * LANES), BF16),
        compiler_params=_params(("arbitrary", "arbitrary", "arbitrary")),
        name="attn",
    )(q, k, v)


def _softplus(z):
    return jnp.maximum(z, 0.0) + jnp.log(1.0 + jnp.exp(-jnp.abs(z)))


def _rwprep_kernel(p_ref, prev_ref, mu_ref, wd_ref, db_ref, wa_ref, ab_ref, wg_ref, kk_ref, ka_ref, seg_ref,
                   r_ref, lw_ref, k_ref, v_ref, kkn_ref, b_ref, g_ref):
    p = p_ref[...]
    tm = p.shape[0]
    first = pl.program_id(1) == 0
    last_prev = jnp.where(first, 0.0, prev_ref[7:8, :])
    rows = lax.broadcasted_iota(jnp.int32, p.shape, 0)
    prev = jnp.where(rows == 0, last_prev, pltpu.roll(p, 1, 0))
    p = p + (prev - p) * mu_ref[...]
    d = RWKV_DIM
    r = p[:, 0:d]
    k = p[:, d:2 * d]
    v = p[:, 2 * d:3 * d]
    o = 3 * d
    wd = p[:, o:o + DECAY_LORA]
    ad = p[:, o + DECAY_LORA:o + DECAY_LORA + AAA_LORA]
    gd = p[:, o + DECAY_LORA + AAA_LORA:]
    w = -_softplus(-(db_ref[...] + _mm3(jnp.tanh(wd), wd_ref[...]))) - 0.5
    a = jax.nn.sigmoid(ab_ref[...] + _mm3(ad, wa_ref[...]))
    kk = k * kk_ref[...]
    n2 = _mm_lx(kk * kk, seg_ref[...])
    kk = kk / jnp.maximum(jnp.sqrt(n2), 1e-12)
    r_ref[...] = r
    lw_ref[...] = -jnp.exp(w)
    k_ref[...] = k * (1.0 + (a - 1.0) * ka_ref[...])
    v_ref[...] = v
    kkn_ref[...] = kk
    b_ref[...] = kk * a
    g_ref[...] = _mm(jax.nn.sigmoid(gd), wg_ref[...])
    del tm


def _rwprep(rw, consts, batch, seq, tm):
    t = rw.shape[0]
    per = seq // tm
    row = pl.BlockSpec((tm, RWKV_DIM), lambda b, s: (b * per + s, 0))
    shp = jax.ShapeDtypeStruct((t, RWKV_DIM), F32)
    return pl.pallas_call(
        _rwprep_kernel,
        grid=(batch, per),
        in_specs=[pl.BlockSpec((tm, RWKV_IN), lambda b, s: (b * per + s, 0)),
                  pl.BlockSpec((8, RWKV_IN), lambda b, s: (jnp.maximum((b * per + s) * (tm // 8) - 1, 0), 0))]
                 + [_full(c.shape) for c in consts],
        out_specs=[row] * 7,
        out_shape=[shp] * 7,
        compiler_params=_params(("arbitrary", "arbitrary")),
        name="rwprep",
    )(rw, rw, *consts)


def _chunk_math(r, lw, k, v, kk, b, mm):
    c = r.shape[0]
    row = lax.broadcasted_iota(jnp.int32, (c, c), 0)
    col = lax.broadcasted_iota(jnp.int32, (c, c), 1)
    incl = row >= col
    strict = row > col
    tri = incl.astype(BF16)
    l_hi = lw.astype(BF16)
    l_r1 = lw - l_hi.astype(F32)
    l_mid = l_r1.astype(BF16)
    l_lo = (l_r1 - l_mid.astype(F32)).astype(BF16)
    cum = _dg(tri, l_hi) + (_dg(tri, l_mid) + _dg(tri, l_lo))
    tot = cum[c - 1:c, :]
    e_pos = jnp.exp(cum)
    e_neg = jnp.exp(-cum)
    e_rem = jnp.exp(tot - cum)
    kk_t = kk * jnp.exp(cum - lw)
    b_t = b * e_neg
    k_t = k * e_neg
    r_t = r * e_pos
    b_h = b * e_rem
    k_h = k * e_rem
    lane = lax.broadcasted_iota(jnp.int32, (1, LANES), 1)
    masks = [(lane < 64).astype(F32), (lane >= 64).astype(F32)]
    n_pairs = r.shape[1] // LANES
    heads = [(p, hh) for p in range(n_pairs) for hh in range(2)]
    nh = len(heads)
    sl = lambda a, p: a[:, p * LANES:(p + 1) * LANES]

    kk_m = [sl(kk_t, p) * masks[hh] for p, hh in heads]
    r_m = [sl(r_t, p) * masks[hh] for p, hh in heads]
    rhs = [jnp.concatenate([sl(b_t, p), sl(k_t, p)], axis=0) for p in range(n_pairs)]
    big = [mm(jnp.concatenate([kk_m[i], r_m[i]], axis=0), rhs[p], NT) for i, (p, _) in enumerate(heads)]
    a_ab = [jnp.where(strict, g_[:c, :c], 0.0) for g_ in big]
    b_rb = [jnp.where(incl, g_[c:, :c], 0.0) for g_ in big]
    av = [mm(jnp.concatenate([jnp.where(strict, big[i][:c, c:], 0.0), jnp.where(incl, big[i][c:, c:], 0.0)], axis=0),
             sl(v, p)) for i, (p, _) in enumerate(heads)]
    z = [jnp.concatenate([kk_m[i], av[i][:c]], axis=1) for i in range(nh)]
    n = a_ab
    z = [z[i] - mm(n[i], z[i]) for i in range(nh)]
    for _ in range(c.bit_length() - 2):
        n = [mm(a, a) for a in n]
        z = [z[i] + mm(n[i], z[i]) for i in range(nh)]
    p12 = [jnp.concatenate([z[i][:, :LANES], z[i][:, LANES:] * masks[hh]], axis=1) for i, (_, hh) in enumerate(heads)]
    bp = [mm(b_rb[i], p12[i]) for i in range(nh)]
    q1h = [r_m[i] - bp[i][:, :LANES] for i in range(nh)]
    q2h = [(av[i][c:] - bp[i][:, LANES:]) * masks[hh] for i, (_, hh) in enumerate(heads)]
    q1 = jnp.concatenate([q1h[2 * p] + q1h[2 * p + 1] for p in range(n_pairs)], axis=1)
    q2 = jnp.concatenate([q2h[2 * p] + q2h[2 * p + 1] for p in range(n_pairs)], axis=1)
    r128 = lax.broadcasted_iota(jnp.int32, (LANES, LANES), 0)
    c128 = lax.broadcasted_iota(jnp.int32, (LANES, LANES), 1)
    same = (r128 >= 64) == (c128 >= 64)
    btp = [mm(sl(b_h, p).T, p12[2 * p] + p12[2 * p + 1]) for p in range(n_pairs)]
    ktv = [mm(sl(k_h, p).T, sl(v, p)) for p in range(n_pairs)]
    e_tot = jnp.exp(tot)
    g = [jnp.where(r128 == c128, sl(e_tot, p), 0.0) - jnp.where(same, btp[p][:, :LANES], 0.0) for p in range(n_pairs)]
    hmat = [jnp.where(same, ktv[p] - btp[p][:, LANES:], 0.0) for p in range(n_pairs)]
    return q1, q2, g, hmat


def _rwchunk_kernel(r_ref, lw_ref, k_ref, v_ref, kk_ref, b_ref, q1_ref, q2_ref, g_ref, h_ref):
    q1, q2, g, hmat = _chunk_math(r_ref[...], lw_ref[...], k_ref[...], v_ref[...], kk_ref[...], b_ref[...], _mm)
    q1_ref[...] = q1
    q2_ref[...] = q2
    for p in range(len(g)):
        g_ref[0, p] = g[p]
        h_ref[0, p] = hmat[p]


def _rwchunk(r, lw, k, v, kk, b, c):
    t = r.shape[0]
    pairs = RWKV_DIM // LANES
    blk = pl.BlockSpec((c, RWKV_DIM), lambda i: (i, 0))
    sq = pl.BlockSpec((1, pairs, LANES, LANES), lambda i: (i, 0, 0, 0))
    return pl.pallas_call(
        _rwchunk_kernel,
        grid=(t // c,),
        in_specs=[blk] * 6,
        out_specs=[blk, blk, sq, sq],
        out_shape=[jax.ShapeDtypeStruct((t, RWKV_DIM), F32)] * 2
                  + [jax.ShapeDtypeStruct((t // c, pairs, LANES, LANES), F32)] * 2,
        compiler_params=_params(("arbitrary",)),
        name="rwchunk",
    )(r, lw, k, v, kk, b)


def _rwscan_kernel(q1_ref, q2_ref, g_ref, h_ref, y_ref, s_ref):
    @pl.when(pl.program_id(1) == 0)
    def _():
        s_ref[...] = jnp.zeros_like(s_ref)

    for p in range(RWKV_DIM // LANES):
        sl = slice(p * LANES, (p + 1) * LANES)
        s = s_ref[p]
        y_ref[:, sl] = _mm3(q1_ref[:, sl], s) + q2_ref[:, sl]
        s_ref[p] = _mm3(g_ref[0, p], s) + h_ref[0, p]


def _rwscan(q1, q2, g, hmat, batch, seq, c):
    t = q1.shape[0]
    per = seq // c
    pairs = RWKV_DIM // LANES
    blk = pl.BlockSpec((c, RWKV_DIM), lambda b, i: (b * per + i, 0))
    sq = pl.BlockSpec((1, pairs, LANES, LANES), lambda b, i: (b * per + i, 0, 0, 0))
    return pl.pallas_call(
        _rwscan_kernel,
        grid=(batch, per),
        in_specs=[blk, blk, sq, sq],
        out_specs=blk,
        out_shape=jax.ShapeDtypeStruct((t, RWKV_DIM), F32),
        scratch_shapes=[pltpu.VMEM((pairs, LANES, LANES), F32)],
        compiler_params=_params(("arbitrary", "arbitrary")),
        name="rwscan",
    )(q1, q2, g, hmat)


def _post_kernel(y_ref, r_ref, k_ref, v_ref, g_ref, om_ref, gate_ref, x_ref, mod_ref, rk_ref, lnw_ref, lnb_ref,
                 segm_ref, wom_ref, wor_ref, wout_ref, nf_ref, x1_ref, h2_ref, h2bt_ref):
    segm = segm_ref[...]
    y = y_ref[...]
    mu = _mm_lx(y, segm)
    dv = y - mu
    var = _mm_lx(dv * dv, segm)
    yn = dv * lax.rsqrt(var + GN_EPS) * lnw_ref[...] + lnb_ref[...]
    bonus = _mm_lx(r_ref[...] * k_ref[...] * rk_ref[...], segm) * float(RWKV_HEAD)
    yy = (yn + bonus * v_ref[...]) * g_ref[...]
    o_r = _mm(yy, wor_ref[...])
    o_m = _dg(om_ref[0], wom_ref[...])
    d = o_m.shape[1]
    gm = gate_ref[:, 0:d].astype(F32)
    gr = gate_ref[:, d:2 * d].astype(F32)
    mix = _mm(gm * o_m + gr * o_r, wout_ref[...])
    x1 = x_ref[...] + mod_ref[0, 2:3, :] * mix
    x1_ref[...] = x1
    h2 = _norm_mod(x1, nf_ref[...], mod_ref[0, 3:4, :], mod_ref[0, 4:5, :])
    h2_ref[...] = h2
    h2bt_ref[...] = h2.T.astype(BF16)


def _post(y, r, k, v, g, o_mla, gates, x2, mod, consts, batch, seq, tm):
    t, d = x2.shape
    per = seq // tm
    row = lambda w: pl.BlockSpec((tm, w), lambda b, s: (b * per + s, 0))
    return pl.pallas_call(
        _post_kernel,
        grid=(batch, per),
        in_specs=[row(RWKV_DIM)] * 5
                 + [pl.BlockSpec((1, tm, o_mla.shape[2]), lambda b, s: (b, s, 0)),
                    row(gates.shape[1]), row(d),
                    pl.BlockSpec((1, 8, d), lambda b, s: (b, 0, 0))]
                 + [_full(c.shape) for c in consts],
        out_specs=[row(d), row(d), pl.BlockSpec((d, tm), lambda b, s: (0, b * per + s))],
        out_shape=[jax.ShapeDtypeStruct((t, d), F32), jax.ShapeDtypeStruct((t, d), F32),
                   jax.ShapeDtypeStruct((d, t), BF16)],
        compiler_params=_params(("arbitrary", "arbitrary")),
        name="post",
    )(y, r, k, v, g, o_mla, gates, x2, mod, *consts)


def _cand_pairs(n):
    return [(a, b) for a in range(n) for b in range(n) if (a + 1) * (b + 1) <= n]


def _sort_network(n):
    def merge(lo, hi, r):
        step = r * 2
        if step < hi - lo:
            yield from merge(lo, hi, step)
            yield from merge(lo + r, hi, step)
            yield from [(i, i + r) for i in range(lo + r, hi - r, step)]
        else:
            yield (lo, lo + r)

    def sort(lo, hi):
        if hi - lo >= 1:
            mid = lo + (hi - lo) // 2
            yield from sort(lo, mid)
            yield from sort(mid + 1, hi)
            yield from merge(lo, hi, 1)

    return tuple(sort(0, n - 1))


def _top_values(s, n):
    vals = []
    cur = s
    for i in range(n):
        m = jnp.max(cur, axis=0, keepdims=True)
        vals.append(m)
        if i + 1 < n:
            cur = jnp.where(cur >= m, NEG, cur)
    return vals


def _top_values_sorted(mats, n):
    cols = [[s[k * 8:(k + 1) * 8] for k in range(s.shape[0] // 8)] for s in mats]
    for i, j in _sort_network(len(cols[0])):
        for c in cols:
            c[i], c[j] = jnp.maximum(c[i], c[j]), jnp.minimum(c[i], c[j])
    vals = [[] for _ in mats]
    for r in range(n):
        left = n - 1 - r
        for c, out in zip(cols, vals):
            m = jnp.max(c[0], axis=0, keepdims=True)
            out.append(m)
            if left:
                hit = c[0] >= m
                for k in range(min(len(c), left)):
                    below = c[k + 1] if k + 1 < len(c) else NEG
                    c[k] = jnp.where(hit, below, c[k])
    return vals


def _count_greater(x, v):
    c = jnp.zeros(x.shape, F32)
    for b, vb in enumerate(v):
        c = jnp.where(vb > x, float(b + 1), c)
    return c


def _route_kernel(h_ref, wqh_ref, wql_ref, k1_ref, k2_ref, cnt_ref, e1_ref, rk_ref, e2_ref):
    hh, hl = _split(h_ref[...])
    k1 = k1_ref[...]
    k2 = k2_ref[...]
    nk = TOPK + 1
    pairs = _cand_pairs(nk)
    s1, s2 = [], []
    for h in range(PEER_HEADS):
        rows = slice(2 * N_KEYS * h, 2 * N_KEYS * (h + 1))
        wh = wqh_ref[rows, :]
        q = _dg(wh, hh, NT) + (_dg(wh, hl, NT) + _dg(wql_ref[rows, :], hh, NT))
        s1.append(_mm3(k1, q[:N_KEYS]))
        s2.append(_mm3(k2, q[N_KEYS:]))
    tops = _top_values_sorted(s1 + s2, nk)
    for h in range(PEER_HEADS):
        v1, v2 = tops[h], tops[PEER_HEADS + h]
        pad = [jnp.full_like(v1[0], NEG)] * (-len(pairs) % 8)
        cand = jnp.concatenate([v1[a] + v2[b] for a, b in pairs] + pad, axis=0)
        top = _top_values(cand, nk)
        z = sum(jnp.exp(top[i] - top[0]) for i in range(1, TOPK)) + 1.0
        tau = 0.5 * (top[TOPK - 1] + top[TOPK])
        cnt_ref[h] = _count_greater(tau - s1[h], v2[:TOPK])
        e1_ref[h] = 0.5 * jnp.exp(s1[h] - v1[0]) / z
        rk_ref[h] = _count_greater(s2[h], v2[:TOPK]).astype(BF16)
        e2_ref[h] = jnp.exp(s2[h] - v2[0]).astype(BF16)


def _route(h2, wq_hi, wq_lo, k1, k2, tm):
    t, d = h2.shape
    blk = pl.BlockSpec((PEER_HEADS, N_KEYS, tm), lambda i: (0, 0, i))
    wide = jax.ShapeDtypeStruct((PEER_HEADS, N_KEYS, t), F32)
    half = jax.ShapeDtypeStruct((PEER_HEADS, N_KEYS, t), BF16)
    return pl.pallas_call(
        _route_kernel,
        grid=(t // tm,),
        in_specs=[pl.BlockSpec((tm, d), lambda i: (i, 0)), _full(wq_hi.shape), _full(wq_lo.shape),
                  _full(k1.shape), _full(k2.shape)],
        out_specs=[blk] * 4,
        out_shape=[wide, wide, half, half],
        compiler_params=_params(("arbitrary",)),
        name="route",
    )(h2, wq_hi, wq_lo, k1, k2)


def _gelu2(x):
    return x * (1.0 + jnp.tanh(x * (0.7978845608028654 + 0.035677408136300125 * (x * x))))


def _bf16_rows(row):
    tile = jnp.broadcast_to(row, (16, row.shape[1])).astype(BF16)
    return jnp.concatenate([tile] * (N_KEYS // 16), axis=0)


def _experts_kernel(ht_ref, dn_ref, up_ref, cnt_ref, e1_ref, rk_ref, e2_ref, x1_ref, mod_ref, o_ref,
                    acc_ref, act0, act1, ga0, ga1, *, rows, nb):
    g = pl.program_id(0)
    last = pl.num_programs(0) - 3
    p_a = jnp.maximum(g - 2, 0)
    blk_v = jnp.clip(g - 1, 0, last) % nb
    zero = jnp.zeros((), BF16)

    @pl.when(g == 0)
    def _():
        act1[...] = jnp.zeros_like(act1)
        ga1[...] = jnp.zeros_like(ga1)

    @pl.when(p_a % nb == 0)
    def _():
        acc_ref[...] = jnp.zeros_like(acc_ref)

    def stages(act_w, act_r, ga_w, ga_r):
        d = acc_ref.shape[0]
        eb = act_w.shape[0]
        mxu = []
        for j in range(rows // 2):
            ar = slice(j * (d // (rows // 2)), (j + 1) * (d // (rows // 2)))
            er = slice(j * (eb // (rows // 2)), (j + 1) * (eb // (rows // 2)))
            mxu.append(("proj", ar))
            mxu.append(("act", er))
        for r0 in range(rows):
            kind, sl_ = mxu[r0]
            if kind == "proj":
                acc_ref[sl_, :] += _dg(up_ref[sl_, :], ga_r[...])
            else:
                act_w[sl_, :] = _dg(dn_ref[sl_, :], ht_ref[...]).astype(BF16)
            rs = slice(r0 * N_KEYS, (r0 + 1) * N_KEYS)
            i = blk_v * rows + r0
            gsum = None
            for h in range(PEER_HEADS):
                cnt = _bf16_rows(cnt_ref[h, pl.ds(i, 1), :])
                e1 = _bf16_rows(e1_ref[h, pl.ds(i, 1), :])
                term = e1 * jnp.where(rk_ref[h] < cnt, e2_ref[h], zero)
                gsum = term if gsum is None else gsum + term
            ga_w[rs, :] = gsum * _gelu2(act_r[rs, :])

    @pl.when(g % 2 == 0)
    def _():
        stages(act0, act1, ga0, ga1)

    @pl.when(g % 2 == 1)
    def _():
        stages(act1, act0, ga1, ga0)

    @pl.when((p_a % nb == nb - 1) & (g >= 2))
    def _():
        o_ref[...] = x1_ref[...] + mod_ref[0, 5:6, :] * acc_ref[...].T


def _experts(h2bt, down, up_t, cnt, e1, rk, e2, x1, mod, seq, tm, rows):
    d, t = h2bt.shape
    ne = down.shape[0]
    eb = rows * N_KEYS
    nb = ne // eb
    per = seq // tm
    pairs = (t // tm) * nb
    p_m = lambda g: jnp.minimum(g, pairs - 1)
    p_v = lambda g: jnp.clip(g - 1, 0, pairs - 1)
    p_a = lambda g: jnp.maximum(g - 2, 0)
    rt = pl.BlockSpec((PEER_HEADS, N_KEYS, tm), lambda g: (0, 0, p_v(g) // nb))
    return pl.pallas_call(
        functools.partial(_experts_kernel, rows=rows, nb=nb),
        grid=(pairs + 2,),
        in_specs=[pl.BlockSpec((d, tm), lambda g: (0, p_m(g) // nb)),
                  pl.BlockSpec((eb, d), lambda g: (p_m(g) % nb, 0)),
                  pl.BlockSpec((d, eb), lambda g: (0, p_a(g) % nb)),
                  rt, rt, rt, rt,
                  pl.BlockSpec((tm, d), lambda g: (p_a(g) // nb, 0)),
                  pl.BlockSpec((1, 8, d), lambda g: (p_a(g) // nb // per, 0, 0))],
        out_specs=pl.BlockSpec((tm, d), lambda g: (p_a(g) // nb, 0)),
        out_shape=jax.ShapeDtypeStruct((t, d), F32),
        scratch_shapes=[pltpu.VMEM((d, tm), F32)] + [pltpu.VMEM((eb, tm), BF16)] * 4,
        compiler_params=_params(("arbitrary",)),
        name="experts",
    )(h2bt, down, up_t, cnt, e1, rk, e2, x1, mod)


def _slab_consts():
    i = jnp.arange(LANES)
    seg = jnp.where(i < QK_NOPE, 0, jnp.where(i < QK_NOPE + QK_ROPE, 1, 2))
    same = (seg[:, None] == seg[None, :]) & (seg[:, None] < 2)
    width = jnp.where(seg == 0, QK_NOPE, QK_ROPE).astype(F32)
    ind = jnp.where(same, 1.0 / width[None, :], 0.0).astype(BF16)
    half = QK_ROPE // 2
    src = jnp.where(i < QK_NOPE + half, i + half, i - half)
    sign = jnp.where(i < QK_NOPE + half, -1.0, 1.0)
    rot = jnp.where((seg[None, :] == 1) & (i[:, None] == src[None, :]), sign[None, :], 0.0).astype(BF16)
    inv = ROPE_THETA ** (-jnp.arange(half, dtype=F32) / half)
    freq = jnp.zeros((LANES,), F32).at[QK_NOPE:QK_NOPE + QK_ROPE].set(jnp.concatenate([inv, inv]))
    return ind, rot, freq.reshape(1, LANES)


def _pad_slab(g):
    return jnp.zeros((1, LANES), F32).at[0, :g.shape[0]].set(g)


def kernel(x, c, positions, w_ada, b_ada, norm_mix, w_in, mu_shift, q_a_norm, w_uq, kv_a_norm, w_ukv, q_norm, k_norm,
           w_o_mla, w_decay_up, decay_base, w_aaa_up, aaa_base, w_gate_up, k_k, k_a, r_k, ln_x_w, ln_x_b, w_o_rwkv,
           w_out, norm_ffn, w_query, sub_keys1, sub_keys2, expert_down, expert_up):
    batch, seq, d = x.shape
    t = batch * seq
    depth = w_ada.shape[0]
    tm = min(256, seq)
    chunk = min(128, seq)
    xs = x.reshape(t, d)
    pos = positions.reshape(t, 1)
    ind, rot, freq = _slab_consts()
    lane512 = jnp.arange(RWKV_DIM) // RWKV_HEAD
    seg_sum = (lane512[:, None] == lane512[None, :]).astype(BF16)
    seg_mean = (seg_sum.astype(F32) / RWKV_HEAD).astype(BF16)

    for l in range(depth):
        wi = w_in[l]
        mla_in = Q_LORA + KV_LORA + QK_ROPE
        kr_cols = jnp.zeros((d, LANES), F32).at[:, QK_NOPE:QK_NOPE + QK_ROPE].set(wi[:, Q_LORA + KV_LORA:mla_in])
        w_cat = jnp.concatenate([wi[:, :Q_LORA + KV_LORA], kr_cols, wi[:, mla_in:]], axis=1).astype(BF16)
        wuq = jnp.pad(w_uq[l].reshape(Q_LORA, MLA_HEADS, QK_NOPE + QK_ROPE),
                      ((0, 0), (0, 0), (0, LANES - QK_NOPE - QK_ROPE))).reshape(Q_LORA, MLA_HEADS * LANES).astype(BF16)
        wkv = w_ukv[l].reshape(KV_LORA, MLA_HEADS, QK_NOPE + V_HEAD)
        zeros = jnp.zeros((KV_LORA, MLA_HEADS, LANES - QK_NOPE), F32)
        wuk = jnp.concatenate([wkv[..., :QK_NOPE], zeros], axis=-1).reshape(KV_LORA, MLA_HEADS * LANES).astype(BF16)
        vv = wkv[..., QK_NOPE:]
        odd = (jnp.arange(MLA_HEADS) % 2 == 1)[None, :, None]
        wuv = jnp.concatenate([jnp.where(odd, 0.0, vv), jnp.where(odd, vv, 0.0)], axis=-1)
        wuv = wuv.reshape(KV_LORA, MLA_HEADS * LANES).astype(BF16)
        row = lambda a: a.reshape(1, -1)

        c8 = jnp.pad(c, ((0, 8 - batch % 8 if batch % 8 else 0), (0, 0)))
        mod = _ada(c8, w_ada[l], b_ada[l])[:batch].reshape(batch, N_MOD, d)
        mod = jnp.pad(mod, ((0, 0), (0, 8 - N_MOD), (0, 0)))

        qa, kva, kr, rw, gates = _inproj(xs, mod, norm_mix[l], w_cat, seq, tm)

        mla_consts = (row(q_a_norm[l]), row(kv_a_norm[l]), wuq, wuk, wuv, _pad_slab(q_norm[l]), _pad_slab(k_norm[l]),
                      freq, ind, rot)
        qh, kh, vh = _mlaprep(qa, kva, kr, pos, mla_consts, batch, seq, tm)
        o_mla = _attention(qh, kh, vh, min(512, seq))

        rw_consts = (row(mu_shift[l]), w_decay_up[l], row(decay_base[l]), w_aaa_up[l], row(aaa_base[l]),
                     w_gate_up[l], row(k_k[l]), row(k_a[l]), seg_sum)
        r, lw, kk_, v, kkn, bb, g = _rwprep(rw, rw_consts, batch, seq, tm)
        q1, q2, gmat, hmat = _rwchunk(r, lw, kk_, v, kkn, bb, chunk)
        y = _rwscan(q1, q2, gmat, hmat, batch, seq, chunk)

        post_consts = (row(r_k[l]), row(ln_x_w[l]), row(ln_x_b[l]), seg_mean, w_o_mla[l].astype(BF16),
                       w_o_rwkv[l].astype(BF16), w_out[l].astype(BF16), row(norm_ffn[l]))
        x1, h2, h2bt = _post(y, r, kk_, v, g, o_mla, gates, xs, mod, post_consts, batch, seq, tm)

        wq_t = w_query[l].T
        wq_hi = wq_t.astype(BF16)
        wq_lo = (wq_t - wq_hi.astype(F32)).astype(BF16)
        tme = min(512, seq)
        cnt, e1, rk, e2 = _route(h2, wq_hi, wq_lo, sub_keys1[l], sub_keys2[l], tme)
        xs = _experts(h2bt, expert_down[l].astype(BF16), expert_up[l].T.astype(BF16), cnt, e1, rk, e2, x1, mod,
                      seq, tme, 8)
    return xs.reshape(batch, seq, d)
```

```python
import functools

import jax
import jax.numpy as jnp
from jax import lax
from jax.experimental import pallas as pl
from jax.experimental.pallas import tpu as pltpu

F32 = jnp.float32
BF16 = jnp.bfloat16

EPS = 1e-6
GN_EPS = 64e-5
ROPE_THETA = 10000.0
N_MOD = 6
MLA_HEADS = 8
QK_NOPE = 64
QK_ROPE = 32
V_HEAD = 64
Q_LORA = 256
KV_LORA = 128
RWKV_HEADS = 8
RWKV_HEAD = 64
RWKV_DIM = RWKV_HEADS * RWKV_HEAD
DECAY_LORA = 64
AAA_LORA = 64
GATE_LORA = 128
PEER_HEADS = 8
N_KEYS = 128
TOPK = 16

LANES = 128
VMEM_LIMIT = 56 * 1024 * 1024
NEG = -3.0e38

_ONES_LANE = (V_HEAD, 0)

NN = (((1,), (0,)), ((), ()))
NT = (((1,), (1,)), ((), ()))


def _dg(a, b, dims=NN):
    return lax.dot_general(a, b, dims, preferred_element_type=F32)


def _mm(a, b, dims=NN):
    return _dg(a.astype(BF16), b.astype(BF16), dims)


def _split(a):
    hi = a.astype(BF16)
    lo = (a - hi.astype(F32)).astype(BF16)
    return hi, lo


def _mm3(a, b, dims=NN):
    ah, al = _split(a)
    bh, bl = _split(b)
    return _dg(ah, bh, dims) + (_dg(ah, bl, dims) + _dg(al, bh, dims))


def _mm_lx(a, b_exact, dims=NN):
    ah, al = _split(a)
    return _dg(ah, b_exact, dims) + _dg(al, b_exact, dims)


def _params(sem):
    return pltpu.CompilerParams(dimension_semantics=sem, vmem_limit_bytes=VMEM_LIMIT)


def _full(shape):
    nd = len(shape)
    return pl.BlockSpec(shape, lambda *_: (0,) * nd)


def _ada_kernel(c_ref, w_ref, b_ref, o_ref):
    c = c_ref[...]
    s = c * jax.nn.sigmoid(c)
    o_ref[...] = _mm3(s, w_ref[...]) + b_ref[...]


def _ada(c8, w_ada, b_ada):
    d = c8.shape[1]
    n = w_ada.shape[1]
    bn = 1024
    return pl.pallas_call(
        _ada_kernel,
        grid=(n // bn,),
        in_specs=[_full(c8.shape),
                  pl.BlockSpec((d, bn), lambda j: (0, j)),
                  pl.BlockSpec((1, bn), lambda j: (0, j))],
        out_specs=pl.BlockSpec((c8.shape[0], bn), lambda j: (0, j)),
        out_shape=jax.ShapeDtypeStruct((c8.shape[0], n), F32),
        compiler_params=_params(("arbitrary",)),
        name="ada",
    )(c8, w_ada, b_ada.reshape(1, n))


_SEG = (0, Q_LORA, Q_LORA + KV_LORA, Q_LORA + KV_LORA + LANES)
RWKV_IN = 3 * RWKV_DIM + DECAY_LORA + AAA_LORA + GATE_LORA


def _norm_mod(x, g, shift, scale):
    ms = jnp.mean(x * x, axis=-1, keepdims=True)
    h = x * lax.rsqrt(ms + EPS) * g
    return h * (1.0 + scale) + shift


def _inproj_kernel(x_ref, mod_ref, g_ref, w_ref, qa_ref, kva_ref, kr_ref, rw_ref, gate_ref):
    h = _norm_mod(x_ref[...], g_ref[...], mod_ref[0, 0:1, :], mod_ref[0, 1:2, :])
    hb = h.astype(BF16)
    o0, o1, o2, o3 = _SEG
    o4 = o3 + RWKV_IN
    qa_ref[...] = _dg(hb, w_ref[:, o0:o1])
    kva_ref[...] = _dg(hb, w_ref[:, o1:o2])
    kr_ref[...] = _dg(hb, w_ref[:, o2:o3])
    rw_ref[...] = _dg(hb, w_ref[:, o3:o4])
    gate_ref[...] = jax.nn.sigmoid(_dg(hb, w_ref[:, o4:])).astype(BF16)


def _inproj(x2, mod, norm_mix, w_cat, seq, tm):
    t, d = x2.shape
    per = seq // tm
    n_gate = w_cat.shape[1] - _SEG[3] - RWKV_IN
    row = lambda w: pl.BlockSpec((tm, w), lambda i: (i, 0))
    return pl.pallas_call(
        _inproj_kernel,
        grid=(t // tm,),
        in_specs=[row(d),
                  pl.BlockSpec((1, 8, d), lambda i: (i // per, 0, 0)),
                  _full((1, d)),
                  _full(w_cat.shape)],
        out_specs=[row(Q_LORA), row(KV_LORA), row(LANES), row(RWKV_IN), row(n_gate)],
        out_shape=[jax.ShapeDtypeStruct((t, Q_LORA), F32),
                   jax.ShapeDtypeStruct((t, KV_LORA), F32),
                   jax.ShapeDtypeStruct((t, LANES), F32),
                   jax.ShapeDtypeStruct((t, RWKV_IN), F32),
                   jax.ShapeDtypeStruct((t, n_gate), BF16)],
        compiler_params=_params(("arbitrary",)),
        name="inproj",
    )(x2, mod, norm_mix.reshape(1, d), w_cat)


def _mlaprep_kernel(qa_ref, kva_ref, kr_ref, pos_ref, qan_ref, kvan_ref, wuq_ref, wuk_ref, wuv_ref,
                    qg_ref, kg_ref, freq_ref, ind_ref, rot_ref, q_ref, k_ref, v_ref, *, scale):
    ind = ind_ref[...]
    rot = rot_ref[...]
    ang = pos_ref[...].astype(F32) * freq_ref[...]
    cos = jnp.cos(ang)
    sin = jnp.sin(ang)

    def rms(a, g):
        return a * lax.rsqrt(jnp.mean(a * a, axis=-1, keepdims=True) + EPS) * g

    def head_norm(slab, g):
        ms = _mm_lx(slab * slab, ind)
        return slab * lax.rsqrt(ms + EPS) * g

    def rope(a):
        return a * cos + _dg(a.astype(BF16), rot) * sin

    cq = rms(qa_ref[...], qan_ref[...]).astype(BF16)
    ckv = rms(kva_ref[...], kvan_ref[...]).astype(BF16)
    qg = qg_ref[...]
    kg = kg_ref[...]
    k_pe = rope(head_norm(kr_ref[...], kg))
    for h in range(MLA_HEADS):
        sl = slice(h * LANES, (h + 1) * LANES)
        qh = rope(head_norm(_dg(cq, wuq_ref[:, sl]), qg)) * scale
        q_ref[0, h] = qh.astype(BF16)
        kh = head_norm(_dg(ckv, wuk_ref[:, sl]), kg) + k_pe
        k_ref[0, h] = kh.astype(BF16)
        ones = (lax.broadcasted_iota(jnp.int32, (1, LANES), 1) == _ONES_LANE[h % 2]).astype(F32)
        v_ref[0, h] = (_dg(ckv, wuv_ref[:, sl]) + ones).astype(BF16)


def _mlaprep(qa, kva, kr, pos, consts, batch, seq, tm):
    per = seq // tm
    row = lambda w: pl.BlockSpec((tm, w), lambda b, s: (b * per + s, 0))
    hm = pl.BlockSpec((1, MLA_HEADS, tm, LANES), lambda b, s: (b, 0, s, 0))
    hshape = jax.ShapeDtypeStruct((batch, MLA_HEADS, seq, LANES), BF16)
    return pl.pallas_call(
        functools.partial(_mlaprep_kernel, scale=float((QK_NOPE + QK_ROPE) ** -0.5)),
        grid=(batch, per),
        in_specs=[row(Q_LORA), row(KV_LORA), row(LANES), row(1)] + [_full(c.shape) for c in consts],
        out_specs=[hm, hm, hm],
        out_shape=[hshape, hshape, hshape],
        compiler_params=_params(("arbitrary", "arbitrary")),
        name="mlaprep",
    )(qa, kva, kr, pos, *consts)


def _attn_kernel(q_ref, k_ref, v_ref, o_ref, *, tq):
    qi = pl.program_id(2)
    row = lax.broadcasted_iota(jnp.int32, (tq, tq), 0)
    col = lax.broadcasted_iota(jnp.int32, (tq, tq), 1)
    causal = col <= row

    def block(j, carry, masked):
        new = []
        for hh in range(2):
            m, acc = carry[hh]
            k = k_ref[0, hh, pl.ds(pl.multiple_of(j * tq, tq), tq), :]
            v = v_ref[0, hh, pl.ds(pl.multiple_of(j * tq, tq), tq), :]
            s = _dg(q_ref[0, hh], k, NT)
            if masked:
                s = jnp.where(causal, s, NEG)
            m_new = jnp.maximum(m, jnp.max(s, axis=-1, keepdims=True))
            p = jnp.exp((s - m_new).astype(BF16))
            acc = jnp.exp(m - m_new) * acc + _dg(p, v)
            new.append((m_new, acc))
        return tuple(new)

    one = (jnp.full((tq, 1), NEG, F32), jnp.zeros((tq, LANES), F32))
    carry = lax.fori_loop(0, qi, lambda j, c: block(j, c, False), (one, one))
    (_, acc0), (_, acc1) = block(qi, carry, True)
    l0 = acc0[:, _ONES_LANE[0]:_ONES_LANE[0] + 1]
    l1 = acc1[:, _ONES_LANE[1]:_ONES_LANE[1] + 1]
    lane = lax.broadcasted_iota(jnp.int32, (1, LANES), 1)
    o_ref[0] = jnp.where(lane < V_HEAD, acc0 / l0, acc1 / l1).astype(BF16)


def _attention(q, k, v, tq):
    batch, heads, seq, _ = q.shape
    return pl.pallas_call(
        functools.partial(_attn_kernel, tq=tq),
        grid=(batch, heads // 2, seq // tq),
        in_specs=[pl.BlockSpec((1, 2, tq, LANES), lambda b, p, i: (b, p, i, 0)),
                  pl.BlockSpec((1, 2, seq, LANES), lambda b, p, i: (b, p, 0, 0)),
                  pl.BlockSpec((1, 2, seq, LANES), lambda b, p, i: (b, p, 0, 0))],
        out_specs=pl.BlockSpec((1, tq, LANES), lambda b, p, i: (b, i, p)),
        out_shape=jax.ShapeDtypeStruct((batch, seq, heads // 2 * LANES), BF16),
        compiler_params=_params(("arbitrary", "arbitrary", "arbitrary")),
        name="attn",
    )(q, k, v)


def _softplus(z):
    return jnp.maximum(z, 0.0) + jnp.log(1.0 + jnp.exp(-jnp.abs(z)))


def _rwprep_kernel(p_ref, prev_ref, mu_ref, wd_ref, db_ref, wa_ref, ab_ref, wg_ref, kk_ref, ka_ref, seg_ref,
                   r_ref, lw_ref, k_ref, v_ref, kkn_ref, b_ref, g_ref):
    p = p_ref[...]
    tm = p.shape[0]
    first = pl.program_id(1) == 0
    last_prev = jnp.where(first, 0.0, prev_ref[7:8, :])
    rows = lax.broadcasted_iota(jnp.int32, p.shape, 0)
    prev = jnp.where(rows == 0, last_prev, pltpu.roll(p, 1, 0))
    p = p + (prev - p) * mu_ref[...]
    d = RWKV_DIM
    r = p[:, 0:d]
    k = p[:, d:2 * d]
    v = p[:, 2 * d:3 * d]
    o = 3 * d
    wd = p[:, o:o + DECAY_LORA]
    ad = p[:, o + DECAY_LORA:o + DECAY_LORA + AAA_LORA]
    gd = p[:, o + DECAY_LORA + AAA_LORA:]
    w = -_softplus(-(db_ref[...] + _mm3(jnp.tanh(wd), wd_ref[...]))) - 0.5
    a = jax.nn.sigmoid(ab_ref[...] + _mm3(ad, wa_ref[...]))
    kk = k * kk_ref[...]
    n2 = _mm_lx(kk * kk, seg_ref[...])
    kk = kk / jnp.maximum(jnp.sqrt(n2), 1e-12)
    r_ref[...] = r
    lw_ref[...] = -jnp.exp(w)
    k_ref[...] = k * (1.0 + (a - 1.0) * ka_ref[...])
    v_ref[...] = v
    kkn_ref[...] = kk
    b_ref[...] = kk * a
    g_ref[...] = _mm(jax.nn.sigmoid(gd), wg_ref[...])
    del tm


def _rwprep(rw, consts, batch, seq, tm):
    t = rw.shape[0]
    per = seq // tm
    row = pl.BlockSpec((tm, RWKV_DIM), lambda b, s: (b * per + s, 0))
    shp = jax.ShapeDtypeStruct((t, RWKV_DIM), F32)
    return pl.pallas_call(
        _rwprep_kernel,
        grid=(batch, per),
        in_specs=[pl.BlockSpec((tm, RWKV_IN), lambda b, s: (b * per + s, 0)),
                  pl.BlockSpec((8, RWKV_IN), lambda b, s: (jnp.maximum((b * per + s) * (tm // 8) - 1, 0), 0))]
                 + [_full(c.shape) for c in consts],
        out_specs=[row] * 7,
        out_shape=[shp] * 7,
        compiler_params=_params(("arbitrary", "arbitrary")),
        name="rwprep",
    )(rw, rw, *consts)


def _chunk_math(r, lw, k, v, kk, b, mm):
    c = r.shape[0]
    row = lax.broadcasted_iota(jnp.int32, (c, c), 0)
    col = lax.broadcasted_iota(jnp.int32, (c, c), 1)
    incl = row >= col
    strict = row > col
    tri = incl.astype(BF16)
    l_hi = lw.astype(BF16)
    l_r1 = lw - l_hi.astype(F32)
    l_mid = l_r1.astype(BF16)
    l_lo = (l_r1 - l_mid.astype(F32)).astype(BF16)
    cum = _dg(tri, l_hi) + (_dg(tri, l_mid) + _dg(tri, l_lo))
    tot = cum[c - 1:c, :]
    e_pos = jnp.exp(cum)
    e_neg = jnp.exp(-cum)
    e_rem = jnp.exp(tot - cum)
    kk_t = kk * jnp.exp(cum - lw)
    b_t = b * e_neg
    k_t = k * e_neg
    r_t = r * e_pos
    b_h = b * e_rem
    k_h = k * e_rem
    lane = lax.broadcasted_iota(jnp.int32, (1, LANES), 1)
    masks = [(lane < 64).astype(F32), (lane >= 64).astype(F32)]
    n_pairs = r.shape[1] // LANES
    heads = [(p, hh) for p in range(n_pairs) for hh in range(2)]
    nh = len(heads)
    sl = lambda a, p: a[:, p * LANES:(p + 1) * LANES]

    kk_m = [sl(kk_t, p) * masks[hh] for p, hh in heads]
    r_m = [sl(r_t, p) * masks[hh] for p, hh in heads]
    rhs = [jnp.concatenate([sl(b_t, p), sl(k_t, p)], axis=0) for p in range(n_pairs)]
    big = [mm(jnp.concatenate([kk_m[i], r_m[i]], axis=0), rhs[p], NT) for i, (p, _) in enumerate(heads)]
    a_ab = [jnp.where(strict, g_[:c, :c], 0.0) for g_ in big]
    b_rb = [jnp.where(incl, g_[c:, :c], 0.0) for g_ in big]
    av = [mm(jnp.concatenate([jnp.where(strict, big[i][:c, c:], 0.0), jnp.where(incl, big[i][c:, c:], 0.0)], axis=0),
             sl(v, p)) for i, (p, _) in enumerate(heads)]
    z = [jnp.concatenate([kk_m[i], av[i][:c]], axis=1) for i in range(nh)]
    n = a_ab
    z = [z[i] - mm(n[i], z[i]) for i in range(nh)]
    for _ in range(c.bit_length() - 2):
        n = [mm(a, a) for a in n]
        z = [z[i] + mm(n[i], z[i]) for i in range(nh)]
    p12 = [jnp.concatenate([z[i][:, :LANES], z[i][:, LANES:] * masks[hh]], axis=1) for i, (_, hh) in enumerate(heads)]
    bp = [mm(b_rb[i], p12[i]) for i in range(nh)]
    q1h = [r_m[i] - bp[i][:, :LANES] for i in range(nh)]
    q2h = [(av[i][c:] - bp[i][:, LANES:]) * masks[hh] for i, (_, hh) in enumerate(heads)]
    q1 = jnp.concatenate([q1h[2 * p] + q1h[2 * p + 1] for p in range(n_pairs)], axis=1)
    q2 = jnp.concatenate([q2h[2 * p] + q2h[2 * p + 1] for p in range(n_pairs)], axis=1)
    r128 = lax.broadcasted_iota(jnp.int32, (LANES, LANES), 0)
    c128 = lax.broadcasted_iota(jnp.int32, (LANES, LANES), 1)
    same = (r128 >= 64) == (c128 >= 64)
    btp = [mm(sl(b_h, p).T, p12[2 * p] + p12[2 * p + 1]) for p in range(n_pairs)]
    ktv = [mm(sl(k_h, p).T, sl(v, p)) for p in range(n_pairs)]
    e_tot = jnp.exp(tot)
    g = [jnp.where(r128 == c128, sl(e_tot, p), 0.0) - jnp.where(same, btp[p][:, :LANES], 0.0) for p in range(n_pairs)]
    hmat = [jnp.where(same, ktv[p] - btp[p][:, LANES:], 0.0) for p in range(n_pairs)]
    return q1, q2, g, hmat


def _rwchunk_kernel(r_ref, lw_ref, k_ref, v_ref, kk_ref, b_ref, q1_ref, q2_ref, g_ref, h_ref):
    q1, q2, g, hmat = _chunk_math(r_ref[...], lw_ref[...], k_ref[...], v_ref[...], kk_ref[...], b_ref[...], _mm)
    q1_ref[...] = q1
    q2_ref[...] = q2
    for p in range(len(g)):
        g_ref[0, p] = g[p]
        h_ref[0, p] = hmat[p]


def _rwchunk(r, lw, k, v, kk, b, c):
    t = r.shape[0]
    pairs = RWKV_DIM // LANES
    blk = pl.BlockSpec((c, RWKV_DIM), lambda i: (i, 0))
    sq = pl.BlockSpec((1, pairs, LANES, LANES), lambda i: (i, 0, 0, 0))
    return pl.pallas_call(
        _rwchunk_kernel,
        grid=(t // c,),
        in_specs=[blk] * 6,
        out_specs=[blk, blk, sq, sq],
        out_shape=[jax.ShapeDtypeStruct((t, RWKV_DIM), F32)] * 2
                  + [jax.ShapeDtypeStruct((t // c, pairs, LANES, LANES), F32)] * 2,
        compiler_params=_params(("arbitrary",)),
        name="rwchunk",
    )(r, lw, k, v, kk, b)


def _rwscan_kernel(q1_ref, q2_ref, g_ref, h_ref, y_ref, s_ref):
    @pl.when(pl.program_id(1) == 0)
    def _():
        s_ref[...] = jnp.zeros_like(s_ref)

    for p in range(RWKV_DIM // LANES):
        sl = slice(p * LANES, (p + 1) * LANES)
        s = s_ref[p]
        y_ref[:, sl] = _mm3(q1_ref[:, sl], s) + q2_ref[:, sl]
        s_ref[p] = _mm3(g_ref[0, p], s) + h_ref[0, p]


def _rwscan(q1, q2, g, hmat, batch, seq, c):
    t = q1.shape[0]
    per = seq // c
    pairs = RWKV_DIM // LANES
    blk = pl.BlockSpec((c, RWKV_DIM), lambda b, i: (b * per + i, 0))
    sq = pl.BlockSpec((1, pairs, LANES, LANES), lambda b, i: (b * per + i, 0, 0, 0))
    return pl.pallas_call(
        _rwscan_kernel,
        grid=(batch, per),
        in_specs=[blk, blk, sq, sq],
        out_specs=blk,
        out_shape=jax.ShapeDtypeStruct((t, RWKV_DIM), F32),
        scratch_shapes=[pltpu.VMEM((pairs, LANES, LANES), F32)],
        compiler_params=_params(("arbitrary", "arbitrary")),
        name="rwscan",
    )(q1, q2, g, hmat)


def _post_kernel(y_ref, r_ref, k_ref, v_ref, g_ref, om_ref, gate_ref, x_ref, mod_ref, rk_ref, lnw_ref, lnb_ref,
                 segm_ref, wom_ref, wor_ref, wout_ref, nf_ref, x1_ref, h2_ref, h2bt_ref):
    segm = segm_ref[...]
    y = y_ref[...]
    mu = _mm_lx(y, segm)
    dv = y - mu
    var = _mm_lx(dv * dv, segm)
    yn = dv * lax.rsqrt(var + GN_EPS) * lnw_ref[...] + lnb_ref[...]
    bonus = _mm_lx(r_ref[...] * k_ref[...] * rk_ref[...], segm) * float(RWKV_HEAD)
    yy = (yn + bonus * v_ref[...]) * g_ref[...]
    o_r = _mm(yy, wor_ref[...])
    o_m = _dg(om_ref[0], wom_ref[...])
    d = o_m.shape[1]
    gm = gate_ref[:, 0:d].astype(F32)
    gr = gate_ref[:, d:2 * d].astype(F32)
    mix = _mm(gm * o_m + gr * o_r, wout_ref[...])
    x1 = x_ref[...] + mod_ref[0, 2:3, :] * mix
    x1_ref[...] = x1
    h2 = _norm_mod(x1, nf_ref[...], mod_ref[0, 3:4, :], mod_ref[0, 4:5, :])
    h2_ref[...] = h2
    h2bt_ref[...] = h2.T.astype(BF16)


def _post(y, r, k, v, g, o_mla, gates, x2, mod, consts, batch, seq, tm):
    t, d = x2.shape
    per = seq // tm
    row = lambda w: pl.BlockSpec((tm, w), lambda b, s: (b * per + s, 0))
    return pl.pallas_call(
        _post_kernel,
        grid=(batch, per),
        in_specs=[row(RWKV_DIM)] * 5
                 + [pl.BlockSpec((1, tm, o_mla.shape[2]), lambda b, s: (b, s, 0)),
                    row(gates.shape[1]), row(d),
                    pl.BlockSpec((1, 8, d), lambda b, s: (b, 0, 0))]
                 + [_full(c.shape) for c in consts],
        out_specs=[row(d), row(d), pl.BlockSpec((d, tm), lambda b, s: (0, b * per + s))],
        out_shape=[jax.ShapeDtypeStruct((t, d), F32), jax.ShapeDtypeStruct((t, d), F32),
                   jax.ShapeDtypeStruct((d, t), BF16)],
        compiler_params=_params(("arbitrary", "arbitrary")),
        name="post",
    )(y, r, k, v, g, o_mla, gates, x2, mod, *consts)


def _cand_pairs(n):
    return [(a, b) for a in range(n) for b in range(n) if (a + 1) * (b + 1) <= n]


def _sort_network(n):
    def merge(lo, hi, r):
        step = r * 2
        if step < hi - lo:
            yield from merge(lo, hi, step)
            yield from merge(lo + r, hi, step)
            yield from [(i, i + r) for i in range(lo + r, hi - r, step)]
        else:
            yield (lo, lo + r)

    def sort(lo, hi):
        if hi - lo >= 1:
            mid = lo + (hi - lo) // 2
            yield from sort(lo, mid)
            yield from sort(mid + 1, hi)
            yield from merge(lo, hi, 1)

    return tuple(sort(0, n - 1))


def _top_values(s, n):
    vals = []
    cur = s
    for i in range(n):
        m = jnp.max(cur, axis=0, keepdims=True)
        vals.append(m)
        if i + 1 < n:
            cur = jnp.where(cur >= m, NEG, cur)
    return vals


def _top_values_sorted(mats, n, together=4):
    tm = mats[0].shape[1]
    problems = [(mi, c0) for mi in range(len(mats)) for c0 in range(0, tm, LANES)]
    pieces = [[[] for _ in range(n)] for _ in mats]
    for p0 in range(0, len(problems), together):
        batch = problems[p0:p0 + together]
        cols = [[mats[mi][k * 8:(k + 1) * 8, c0:c0 + LANES] for k in range(mats[mi].shape[0] // 8)]
                for mi, c0 in batch]
        for i, j in _sort_network(len(cols[0])):
            for c in cols:
                c[i], c[j] = jnp.maximum(c[i], c[j]), jnp.minimum(c[i], c[j])
        for r in range(n):
            left = n - 1 - r
            for c, (mi, _) in zip(cols, batch):
                m = jnp.max(c[0], axis=0, keepdims=True)
                pieces[mi][r].append(m)
                if left:
                    hit = c[0] >= m
                    for k in range(min(len(c), left)):
                        below = c[k + 1] if k + 1 < len(c) else NEG
                        c[k] = jnp.where(hit, below, c[k])
    return [[jnp.concatenate(p, axis=1) for p in per_rank] for per_rank in pieces]


def _count_greater(x, v):
    c = jnp.zeros(x.shape, F32)
    for b, vb in enumerate(v):
        c = jnp.where(vb > x, float(b + 1), c)
    return c


def _route_kernel(h_ref, wqh_ref, wql_ref, k1_ref, k2_ref, cnt_ref, e1_ref, rk_ref, e2_ref, hs_ref, sa_ref, sb_ref):
    nk = TOPK + 1
    pairs = _cand_pairs(nk)
    hs_ref[0], hs_ref[1] = _split(h_ref[...])

    def scores(h, s_ref):
        rows = pl.ds(pl.multiple_of(h * 2 * N_KEYS, 2 * N_KEYS), 2 * N_KEYS)
        wh = wqh_ref[rows, :]
        q = _dg(wh, hs_ref[0], NT) + (_dg(wh, hs_ref[1], NT) + _dg(wql_ref[rows, :], hs_ref[0], NT))
        s_ref[0] = _mm3(k1_ref[...], q[:N_KEYS])
        s_ref[1] = _mm3(k2_ref[...], q[N_KEYS:])

    scores(0, sa_ref)

    def two_heads(i, carry):
        select(2 * i, sa_ref, sb_ref)
        select(2 * i + 1, sb_ref, sa_ref)
        return carry

    def select(h, s_ref, next_ref):
        scores(jnp.minimum(h + 1, PEER_HEADS - 1), next_ref)
        s1 = s_ref[0]
        s2 = s_ref[1]
        v1, v2 = _top_values_sorted([s1, s2], nk)
        pad = [jnp.full_like(v1[0], NEG)] * (-len(pairs) % 8)
        cand = jnp.concatenate([v1[a] + v2[b] for a, b in pairs] + pad, axis=0)
        top = _top_values(cand, nk)
        z = sum(jnp.exp(top[i] - top[0]) for i in range(1, TOPK)) + 1.0
        tau = 0.5 * (top[TOPK - 1] + top[TOPK])
        cnt_ref[h] = _count_greater(tau - s1, v2[:TOPK])
        e1_ref[h] = 0.5 * jnp.exp(s1 - v1[0]) / z
        rk_ref[h] = _count_greater(s2, v2[:TOPK]).astype(BF16)
        e2_ref[h] = jnp.exp(s2 - v2[0]).astype(BF16)

    lax.fori_loop(0, PEER_HEADS // 2, two_heads, 0)


def _route(h2, wq_hi, wq_lo, k1, k2, tm):
    t, d = h2.shape
    blk = pl.BlockSpec((PEER_HEADS, N_KEYS, tm), lambda i: (0, 0, i))
    wide = jax.ShapeDtypeStruct((PEER_HEADS, N_KEYS, t), F32)
    half = jax.ShapeDtypeStruct((PEER_HEADS, N_KEYS, t), BF16)
    return pl.pallas_call(
        _route_kernel,
        grid=(t // tm,),
        in_specs=[pl.BlockSpec((tm, d), lambda i: (i, 0)), _full(wq_hi.shape), _full(wq_lo.shape),
                  _full(k1.shape), _full(k2.shape)],
        out_specs=[blk] * 4,
        out_shape=[wide, wide, half, half],
        scratch_shapes=[pltpu.VMEM((2, tm, d), BF16)] + [pltpu.VMEM((2, N_KEYS, tm), F32)] * 2,
        compiler_params=_params(("arbitrary",)),
        name="route",
    )(h2, wq_hi, wq_lo, k1, k2)


def _gelu2(x):
    return x * (1.0 + jnp.tanh(x * (0.7978845608028654 + 0.035677408136300125 * (x * x))))


def _bf16_rows(row):
    tile = jnp.broadcast_to(row, (16, row.shape[1])).astype(BF16)
    return jnp.concatenate([tile] * (N_KEYS // 16), axis=0)


def _experts_kernel(ht_ref, dn_ref, up_ref, cnt_ref, e1_ref, rk_ref, e2_ref, x1_ref, mod_ref, o_ref, acc_ref, ga_ref,
                    *, rows, sub):
    e = pl.program_id(1)
    nb = pl.num_programs(1) - 1
    zero = jnp.zeros((), BF16)
    width = sub * N_KEYS

    @pl.when(e == 0)
    def _():
        acc_ref[...] = jnp.zeros_like(acc_ref)
        ga_ref[1] = jnp.zeros(ga_ref.shape[1:], BF16)

    nsub = rows // sub
    d = acc_ref.shape[0]

    def project_previous(part, parts):
        rs = slice(part * (d // parts), (part + 1) * (d // parts))
        acc_ref[rs, :] += _dg(up_ref[rs, :], ga_ref[(e + 1) % 2])

    @pl.when(e < nb)
    def _():
        raw = {}
        for s in range(nsub + 1):
            if s < nsub:
                raw[s] = _dg(dn_ref[s * width:(s + 1) * width, :], ht_ref[...])
            if s >= 1:
                project_previous(s - 1, nsub)
                act = _gelu2(raw.pop(s - 1).astype(BF16))
                for ii in range(sub):
                    r0 = (s - 1) * sub + ii
                    i = e * rows + r0
                    gsum = None
                    for h in range(PEER_HEADS):
                        cnt = _bf16_rows(cnt_ref[h, pl.ds(i, 1), :])
                        e1 = _bf16_rows(e1_ref[h, pl.ds(i, 1), :])
                        term = e1 * jnp.where(rk_ref[h] < cnt, e2_ref[h], zero)
                        gsum = term if gsum is None else gsum + term
                    ga_ref[e % 2, r0 * N_KEYS:(r0 + 1) * N_KEYS, :] = gsum * act[ii * N_KEYS:(ii + 1) * N_KEYS]

    @pl.when(e == nb)
    def _():
        project_previous(0, 1)
        o_ref[...] = x1_ref[...] + mod_ref[0, 5:6, :] * acc_ref[...].T


def _experts(h2bt, down, up_t, cnt, e1, rk, e2, x1, mod, seq, tm, rows, sub):
    d, t = h2bt.shape
    ne = down.shape[0]
    eb = rows * N_KEYS
    nb = ne // eb
    per = seq // tm
    rt = pl.BlockSpec((PEER_HEADS, N_KEYS, tm), lambda i, e: (0, 0, i))
    return pl.pallas_call(
        functools.partial(_experts_kernel, rows=rows, sub=sub),
        grid=(t // tm, nb + 1),
        in_specs=[pl.BlockSpec((d, tm), lambda i, e: (0, i)),
                  pl.BlockSpec((eb, d), lambda i, e: (jnp.minimum(e, nb - 1), 0)),
                  pl.BlockSpec((d, eb), lambda i, e: (0, jnp.maximum(e - 1, 0))),
                  rt, rt, rt, rt,
                  pl.BlockSpec((tm, d), lambda i, e: (i, 0)),
                  pl.BlockSpec((1, 8, d), lambda i, e: (i // per, 0, 0))],
        out_specs=pl.BlockSpec((tm, d), lambda i, e: (i, 0)),
        out_shape=jax.ShapeDtypeStruct((t, d), F32),
        scratch_shapes=[pltpu.VMEM((d, tm), F32), pltpu.VMEM((2, eb, tm), BF16)],
        compiler_params=_params(("arbitrary", "arbitrary")),
        name="experts",
    )(h2bt, down, up_t, cnt, e1, rk, e2, x1, mod)


def _slab_consts():
    i = jnp.arange(LANES)
    seg = jnp.where(i < QK_NOPE, 0, jnp.where(i < QK_NOPE + QK_ROPE, 1, 2))
    same = (seg[:, None] == seg[None, :]) & (seg[:, None] < 2)
    width = jnp.where(seg == 0, QK_NOPE, QK_ROPE).astype(F32)
    ind = jnp.where(same, 1.0 / width[None, :], 0.0).astype(BF16)
    half = QK_ROPE // 2
    src = jnp.where(i < QK_NOPE + half, i + half, i - half)
    sign = jnp.where(i < QK_NOPE + half, -1.0, 1.0)
    rot = jnp.where((seg[None, :] == 1) & (i[:, None] == src[None, :]), sign[None, :], 0.0).astype(BF16)
    inv = ROPE_THETA ** (-jnp.arange(half, dtype=F32) / half)
    freq = jnp.zeros((LANES,), F32).at[QK_NOPE:QK_NOPE + QK_ROPE].set(jnp.concatenate([inv, inv]))
    return ind, rot, freq.reshape(1, LANES)


def _pad_slab(g):
    return jnp.zeros((1, LANES), F32).at[0, :g.shape[0]].set(g)


def kernel(x, c, positions, w_ada, b_ada, norm_mix, w_in, mu_shift, q_a_norm, w_uq, kv_a_norm, w_ukv, q_norm, k_norm,
           w_o_mla, w_decay_up, decay_base, w_aaa_up, aaa_base, w_gate_up, k_k, k_a, r_k, ln_x_w, ln_x_b, w_o_rwkv,
           w_out, norm_ffn, w_query, sub_keys1, sub_keys2, expert_down, expert_up):
    batch, seq, d = x.shape
    t = batch * seq
    depth = w_ada.shape[0]
    tm = min(256, seq)
    chunk = min(128, seq)
    xs = x.reshape(t, d)
    pos = positions.reshape(t, 1)
    ind, rot, freq = _slab_consts()
    lane512 = jnp.arange(RWKV_DIM) // RWKV_HEAD
    seg_sum = (lane512[:, None] == lane512[None, :]).astype(BF16)
    seg_mean = (seg_sum.astype(F32) / RWKV_HEAD).astype(BF16)

    for l in range(depth):
        wi = w_in[l]
        mla_in = Q_LORA + KV_LORA + QK_ROPE
        kr_cols = jnp.zeros((d, LANES), F32).at[:, QK_NOPE:QK_NOPE + QK_ROPE].set(wi[:, Q_LORA + KV_LORA:mla_in])
        w_cat = jnp.concatenate([wi[:, :Q_LORA + KV_LORA], kr_cols, wi[:, mla_in:]], axis=1).astype(BF16)
        wuq = jnp.pad(w_uq[l].reshape(Q_LORA, MLA_HEADS, QK_NOPE + QK_ROPE),
                      ((0, 0), (0, 0), (0, LANES - QK_NOPE - QK_ROPE))).reshape(Q_LORA, MLA_HEADS * LANES).astype(BF16)
        wkv = w_ukv[l].reshape(KV_LORA, MLA_HEADS, QK_NOPE + V_HEAD)
        zeros = jnp.zeros((KV_LORA, MLA_HEADS, LANES - QK_NOPE), F32)
        wuk = jnp.concatenate([wkv[..., :QK_NOPE], zeros], axis=-1).reshape(KV_LORA, MLA_HEADS * LANES).astype(BF16)
        vv = wkv[..., QK_NOPE:]
        odd = (jnp.arange(MLA_HEADS) % 2 == 1)[None, :, None]
        wuv = jnp.concatenate([jnp.where(odd, 0.0, vv), jnp.where(odd, vv, 0.0)], axis=-1)
        wuv = wuv.reshape(KV_LORA, MLA_HEADS * LANES).astype(BF16)
        row = lambda a: a.reshape(1, -1)

        c8 = jnp.pad(c, ((0, 8 - batch % 8 if batch % 8 else 0), (0, 0)))
        mod = _ada(c8, w_ada[l], b_ada[l])[:batch].reshape(batch, N_MOD, d)
        mod = jnp.pad(mod, ((0, 0), (0, 8 - N_MOD), (0, 0)))

        qa, kva, kr, rw, gates = _inproj(xs, mod, norm_mix[l], w_cat, seq, tm)

        mla_consts = (row(q_a_norm[l]), row(kv_a_norm[l]), wuq, wuk, wuv, _pad_slab(q_norm[l]), _pad_slab(k_norm[l]),
                      freq, ind, rot)
        qh, kh, vh = _mlaprep(qa, kva, kr, pos, mla_consts, batch, seq, tm)
        o_mla = _attention(qh, kh, vh, min(512, seq))

        rw_consts = (row(mu_shift[l]), w_decay_up[l], row(decay_base[l]), w_aaa_up[l], row(aaa_base[l]),
                     w_gate_up[l], row(k_k[l]), row(k_a[l]), seg_sum)
        r, lw, kk_, v, kkn, bb, g = _rwprep(rw, rw_consts, batch, seq, tm)
        q1, q2, gmat, hmat = _rwchunk(r, lw, kk_, v, kkn, bb, chunk)
        y = _rwscan(q1, q2, gmat, hmat, batch, seq, chunk)

        post_consts = (row(r_k[l]), row(ln_x_w[l]), row(ln_x_b[l]), seg_mean, w_o_mla[l].astype(BF16),
                       w_o_rwkv[l].astype(BF16), w_out[l].astype(BF16), row(norm_ffn[l]))
        x1, h2, h2bt = _post(y, r, kk_, v, g, o_mla, gates, xs, mod, post_consts, batch, seq, tm)

        wq_t = w_query[l].T
        wq_hi = wq_t.astype(BF16)
        wq_lo = (wq_t - wq_hi.astype(F32)).astype(BF16)
        tme = min(512, seq)
        cnt, e1, rk, e2 = _route(h2, wq_hi, wq_lo, sub_keys1[l], sub_keys2[l], tme)
        xs = _experts(h2bt, expert_down[l].astype(BF16), expert_up[l].T.astype(BF16), cnt, e1, rk, e2, x1, mod,
                      seq, tme, 8, 2)
    return xs.reshape(batch, seq, d)
```

```python
import functools
from typing import NamedTuple

import jax
import jax.numpy as jnp
from jax import lax
from jax.experimental import pallas as pl
from jax.experimental.pallas import tpu as pltpu

F32 = jnp.float32
BF16 = jnp.bfloat16

EPS = 1e-6
GN_EPS = 64e-5
ROPE_THETA = 10000.0
N_MOD = 6
MLA_HEADS = 8
QK_NOPE = 64
QK_ROPE = 32
V_HEAD = 64
Q_LORA = 256
KV_LORA = 128
RWKV_HEADS = 8
RWKV_HEAD = 64
RWKV_DIM = RWKV_HEADS * RWKV_HEAD
DECAY_LORA = 64
AAA_LORA = 64
GATE_LORA = 128
PEER_HEADS = 8
N_KEYS = 128
TOPK = 16

LANES = 128
VMEM_LIMIT = 56 * 1024 * 1024
NEG = -3.0e38

_ONES_LANE = (V_HEAD, 0)

NN = (((1,), (0,)), ((), ()))
NT = (((1,), (1,)), ((), ()))


def _dg(a, b, dims=NN):
    return lax.dot_general(a, b, dims, preferred_element_type=F32)


def _mm(a, b, dims=NN):
    return _dg(a.astype(BF16), b.astype(BF16), dims)


def _split(a):
    hi = a.astype(BF16)
    lo = (a - hi.astype(F32)).astype(BF16)
    return hi, lo


def _mm3(a, b, dims=NN):
    ah, al = _split(a)
    bh, bl = _split(b)
    return _dg(ah, bh, dims) + (_dg(ah, bl, dims) + _dg(al, bh, dims))


def _mm_lx(a, b_exact, dims=NN):
    ah, al = _split(a)
    return _dg(ah, b_exact, dims) + _dg(al, b_exact, dims)


def _params(sem):
    return pltpu.CompilerParams(dimension_semantics=sem, vmem_limit_bytes=VMEM_LIMIT)


def _full(shape):
    nd = len(shape)
    return pl.BlockSpec(shape, lambda *_: (0,) * nd)


def _ada_kernel(c_ref, w_ref, b_ref, o_ref):
    c = c_ref[...]
    s = c * jax.nn.sigmoid(c)
    o_ref[...] = _mm3(s, w_ref[...]) + b_ref[...]


def _ada(c8, w_ada, b_ada):
    d = c8.shape[1]
    n = w_ada.shape[1]
    bn = 1024
    return pl.pallas_call(
        _ada_kernel,
        grid=(n // bn,),
        in_specs=[_full(c8.shape),
                  pl.BlockSpec((d, bn), lambda j: (0, j)),
                  pl.BlockSpec((1, bn), lambda j: (0, j))],
        out_specs=pl.BlockSpec((c8.shape[0], bn), lambda j: (0, j)),
        out_shape=jax.ShapeDtypeStruct((c8.shape[0], n), F32),
        compiler_params=_params(("arbitrary",)),
        name="ada",
    )(c8, w_ada, b_ada.reshape(1, n))


_SEG = (0, Q_LORA, Q_LORA + KV_LORA, Q_LORA + KV_LORA + LANES)
RWKV_IN = 3 * RWKV_DIM + DECAY_LORA + AAA_LORA + GATE_LORA


def _norm_mod(x, g, shift, scale):
    ms = jnp.mean(x * x, axis=-1, keepdims=True)
    h = x * lax.rsqrt(ms + EPS) * g
    return h * (1.0 + scale) + shift


def _inproj_kernel(x_ref, mod_ref, g_ref, w_ref, qa_ref, kva_ref, kr_ref, rw_ref, gate_ref):
    h = _norm_mod(x_ref[...], g_ref[...], mod_ref[0, 0:1, :], mod_ref[0, 1:2, :])
    hb = h.astype(BF16)
    o0, o1, o2, o3 = _SEG
    o4 = o3 + RWKV_IN
    qa_ref[...] = _dg(hb, w_ref[:, o0:o1])
    kva_ref[...] = _dg(hb, w_ref[:, o1:o2])
    kr_ref[...] = _dg(hb, w_ref[:, o2:o3])
    rw_ref[...] = _dg(hb, w_ref[:, o3:o4])
    gate_ref[...] = jax.nn.sigmoid(_dg(hb, w_ref[:, o4:])).astype(BF16)


def _inproj(x2, mod, norm_mix, w_cat, seq, tm):
    t, d = x2.shape
    per = seq // tm
    n_gate = w_cat.shape[1] - _SEG[3] - RWKV_IN
    row = lambda w: pl.BlockSpec((tm, w), lambda i: (i, 0))
    return pl.pallas_call(
        _inproj_kernel,
        grid=(t // tm,),
        in_specs=[row(d),
                  pl.BlockSpec((1, 8, d), lambda i: (i // per, 0, 0)),
                  _full((1, d)),
                  _full(w_cat.shape)],
        out_specs=[row(Q_LORA), row(KV_LORA), row(LANES), row(RWKV_IN), row(n_gate)],
        out_shape=[jax.ShapeDtypeStruct((t, Q_LORA), F32),
                   jax.ShapeDtypeStruct((t, KV_LORA), F32),
                   jax.ShapeDtypeStruct((t, LANES), F32),
                   jax.ShapeDtypeStruct((t, RWKV_IN), F32),
                   jax.ShapeDtypeStruct((t, n_gate), BF16)],
        compiler_params=_params(("arbitrary",)),
        name="inproj",
    )(x2, mod, norm_mix.reshape(1, d), w_cat)


def _mlaprep_kernel(qa_ref, kva_ref, kr_ref, pos_ref, qan_ref, kvan_ref, wuq_ref, wuk_ref, wuv_ref,
                    qg_ref, kg_ref, freq_ref, ind_ref, rot_ref, q_ref, k_ref, v_ref, *, scale):
    ind = ind_ref[...]
    rot = rot_ref[...]
    ang = pos_ref[...].astype(F32) * freq_ref[...]
    cos = jnp.cos(ang)
    sin = jnp.sin(ang)

    def rms(a, g):
        return a * lax.rsqrt(jnp.mean(a * a, axis=-1, keepdims=True) + EPS) * g

    def head_norm(slab, g):
        ms = _mm_lx(slab * slab, ind)
        return slab * lax.rsqrt(ms + EPS) * g

    def rope(a):
        return a * cos + _dg(a.astype(BF16), rot) * sin

    cq = rms(qa_ref[...], qan_ref[...]).astype(BF16)
    ckv = rms(kva_ref[...], kvan_ref[...]).astype(BF16)
    qg = qg_ref[...]
    kg = kg_ref[...]
    k_pe = rope(head_norm(kr_ref[...], kg))
    for h in range(MLA_HEADS):
        sl = slice(h * LANES, (h + 1) * LANES)
        qh = rope(head_norm(_dg(cq, wuq_ref[:, sl]), qg)) * scale
        q_ref[0, h] = qh.astype(BF16)
        kh = head_norm(_dg(ckv, wuk_ref[:, sl]), kg) + k_pe
        k_ref[0, h] = kh.astype(BF16)
        ones = (lax.broadcasted_iota(jnp.int32, (1, LANES), 1) == _ONES_LANE[h % 2]).astype(F32)
        v_ref[0, h] = (_dg(ckv, wuv_ref[:, sl]) + ones).astype(BF16)


def _mlaprep(qa, kva, kr, pos, consts, batch, seq, tm):
    per = seq // tm
    row = lambda w: pl.BlockSpec((tm, w), lambda b, s: (b * per + s, 0))
    hm = pl.BlockSpec((1, MLA_HEADS, tm, LANES), lambda b, s: (b, 0, s, 0))
    hshape = jax.ShapeDtypeStruct((batch, MLA_HEADS, seq, LANES), BF16)
    return pl.pallas_call(
        functools.partial(_mlaprep_kernel, scale=float((QK_NOPE + QK_ROPE) ** -0.5)),
        grid=(batch, per),
        in_specs=[row(Q_LORA), row(KV_LORA), row(LANES), row(1)] + [_full(c.shape) for c in consts],
        out_specs=[hm, hm, hm],
        out_shape=[hshape, hshape, hshape],
        compiler_params=_params(("arbitrary", "arbitrary")),
        name="mlaprep",
    )(qa, kva, kr, pos, *consts)


def _attn_kernel(q_ref, k_ref, v_ref, o_ref, *, tq, tk):
    qi = pl.program_id(2)
    nh = q_ref.shape[1]
    nd = tq // tk

    def update(hh, m, acc, q, start, mask):
        k = k_ref[0, hh, pl.ds(start, tk), :]
        v = v_ref[0, hh, pl.ds(start, tk), :]
        s = _dg(q, k, NT)
        if mask is not None:
            s = jnp.where(mask, s, NEG)
        m_new = jnp.maximum(m, jnp.max(s, axis=-1, keepdims=True))
        p = jnp.exp((s - m_new).astype(BF16))
        return m_new, jnp.exp(m - m_new) * acc + _dg(p, v)

    def below_diagonal(j, carry):
        start = pl.multiple_of(j * tk, tk)
        return tuple(update(hh, *carry[hh], q_ref[0, hh], start, None) for hh in range(nh))

    one = (jnp.full((tq, 1), NEG, F32), jnp.zeros((tq, LANES), F32))
    carry = lax.fori_loop(0, qi * nd, below_diagonal, (one,) * nh)
    for d in range(nd):
        r0 = d * tk
        row = lax.broadcasted_iota(jnp.int32, (tq - r0, tk), 0)
        col = lax.broadcasted_iota(jnp.int32, (tq - r0, tk), 1)
        start = pl.multiple_of(qi * tq + r0, tk)
        new = []
        for hh in range(nh):
            m, acc = carry[hh]
            m_s, acc_s = update(hh, m[r0:], acc[r0:], q_ref[0, hh, r0:, :], start, col <= row)
            if r0:
                m_s = jnp.concatenate([m[:r0], m_s], axis=0)
                acc_s = jnp.concatenate([acc[:r0], acc_s], axis=0)
            new.append((m_s, acc_s))
        carry = tuple(new)
    accs = [acc for _, acc in carry]
    lane = lax.broadcasted_iota(jnp.int32, (1, LANES), 1)
    for p in range(nh // 2):
        acc0, acc1 = accs[2 * p], accs[2 * p + 1]
        l0 = acc0[:, _ONES_LANE[0]:_ONES_LANE[0] + 1]
        l1 = acc1[:, _ONES_LANE[1]:_ONES_LANE[1] + 1]
        o_ref[0, :, p * LANES:(p + 1) * LANES] = jnp.where(lane < V_HEAD, acc0 / l0, acc1 / l1).astype(BF16)


def _attention(q, k, v, tq, tk, hp):
    batch, heads, seq, _ = q.shape
    return pl.pallas_call(
        functools.partial(_attn_kernel, tq=tq, tk=tk),
        grid=(batch, heads // hp, seq // tq),
        in_specs=[pl.BlockSpec((1, hp, tq, LANES), lambda b, p, i: (b, p, i, 0)),
                  pl.BlockSpec((1, hp, seq, LANES), lambda b, p, i: (b, p, 0, 0)),
                  pl.BlockSpec((1, hp, seq, LANES), lambda b, p, i: (b, p, 0, 0))],
        out_specs=pl.BlockSpec((1, tq, hp // 2 * LANES), lambda b, p, i: (b, i, p)),
        out_shape=jax.ShapeDtypeStruct((batch, seq, heads // 2 * LANES), BF16),
        compiler_params=_params(("arbitrary", "arbitrary", "arbitrary")),
        name="attn",
    )(q, k, v)


def _softplus(z):
    return jnp.maximum(z, 0.0) + jnp.log(1.0 + jnp.exp(-jnp.abs(z)))


def _rwprep_kernel(p_ref, prev_ref, mu_ref, wd_ref, db_ref, wa_ref, ab_ref, wg_ref, kk_ref, ka_ref, seg_ref,
                   r_ref, lw_ref, k_ref, v_ref, kkn_ref, b_ref, g_ref):
    p = p_ref[...]
    first = pl.program_id(1) == 0
    last_prev = jnp.where(first, 0.0, prev_ref[7:8, :])
    rows = lax.broadcasted_iota(jnp.int32, p.shape, 0)
    prev = jnp.where(rows == 0, last_prev, pltpu.roll(p, 1, 0))
    p = p + (prev - p) * mu_ref[...]
    d = RWKV_DIM
    r = p[:, 0:d]
    k = p[:, d:2 * d]
    v = p[:, 2 * d:3 * d]
    o = 3 * d
    wd = p[:, o:o + DECAY_LORA]
    ad = p[:, o + DECAY_LORA:o + DECAY_LORA + AAA_LORA]
    gd = p[:, o + DECAY_LORA + AAA_LORA:]
    w = -_softplus(-(db_ref[...] + _mm3(jnp.tanh(wd), wd_ref[...]))) - 0.5
    a = jax.nn.sigmoid(ab_ref[...] + _mm3(ad, wa_ref[...]))
    kk = k * kk_ref[...]
    n2 = _mm_lx(kk * kk, seg_ref[...])
    kk = kk / jnp.maximum(jnp.sqrt(n2), 1e-12)
    r_ref[...] = r
    lw_ref[...] = -jnp.exp(w)
    k_ref[...] = k * (1.0 + (a - 1.0) * ka_ref[...])
    v_ref[...] = v
    kkn_ref[...] = kk
    b_ref[...] = kk * a
    g_ref[...] = _mm(jax.nn.sigmoid(gd), wg_ref[...])


def _rwprep(rw, consts, batch, seq, tm):
    t = rw.shape[0]
    per = seq // tm
    row = pl.BlockSpec((tm, RWKV_DIM), lambda b, s: (b * per + s, 0))
    shp = jax.ShapeDtypeStruct((t, RWKV_DIM), F32)
    return pl.pallas_call(
        _rwprep_kernel,
        grid=(batch, per),
        in_specs=[pl.BlockSpec((tm, RWKV_IN), lambda b, s: (b * per + s, 0)),
                  pl.BlockSpec((8, RWKV_IN), lambda b, s: (jnp.maximum((b * per + s) * (tm // 8) - 1, 0), 0))]
                 + [_full(c.shape) for c in consts],
        out_specs=[row] * 7,
        out_shape=[shp] * 7,
        compiler_params=_params(("arbitrary", "arbitrary")),
        name="rwprep",
    )(rw, rw, *consts)


def _chunk_math(r, lw, k, v, kk, b, mm):
    c = r.shape[0]
    row = lax.broadcasted_iota(jnp.int32, (c, c), 0)
    col = lax.broadcasted_iota(jnp.int32, (c, c), 1)
    incl = row >= col
    strict = row > col
    tri = incl.astype(BF16)
    l_hi = lw.astype(BF16)
    l_r1 = lw - l_hi.astype(F32)
    l_mid = l_r1.astype(BF16)
    l_lo = (l_r1 - l_mid.astype(F32)).astype(BF16)
    cum = _dg(tri, l_hi) + (_dg(tri, l_mid) + _dg(tri, l_lo))
    tot = cum[c - 1:c, :]
    e_pos = jnp.exp(cum)
    e_neg = jnp.exp(-cum)
    e_rem = jnp.exp(tot - cum)
    kk_t = kk * jnp.exp(cum - lw)
    b_t = b * e_neg
    k_t = k * e_neg
    r_t = r * e_pos
    b_h = b * e_rem
    k_h = k * e_rem
    lane = lax.broadcasted_iota(jnp.int32, (1, LANES), 1)
    masks = [(lane < 64).astype(F32), (lane >= 64).astype(F32)]
    n_pairs = r.shape[1] // LANES
    heads = [(p, hh) for p in range(n_pairs) for hh in range(2)]
    nh = len(heads)
    sl = lambda a, p: a[:, p * LANES:(p + 1) * LANES]

    kk_m = [sl(kk_t, p) * masks[hh] for p, hh in heads]
    r_m = [sl(r_t, p) * masks[hh] for p, hh in heads]
    rhs = [jnp.concatenate([sl(b_t, p), sl(k_t, p)], axis=0) for p in range(n_pairs)]
    big = [mm(jnp.concatenate([kk_m[i], r_m[i]], axis=0), rhs[p], NT) for i, (p, _) in enumerate(heads)]
    a_ab = [jnp.where(strict, g_[:c, :c], 0.0) for g_ in big]
    b_rb = [jnp.where(incl, g_[c:, :c], 0.0) for g_ in big]
    av = [mm(jnp.concatenate([jnp.where(strict, big[i][:c, c:], 0.0), jnp.where(incl, big[i][c:, c:], 0.0)], axis=0),
             sl(v, p)) for i, (p, _) in enumerate(heads)]
    z = [jnp.concatenate([kk_m[i], av[i][:c]], axis=1) for i in range(nh)]
    n = a_ab
    z = [z[i] - mm(n[i], z[i]) for i in range(nh)]
    for _ in range(c.bit_length() - 2):
        n = [mm(a, a) for a in n]
        z = [z[i] + mm(n[i], z[i]) for i in range(nh)]
    p12 = [jnp.concatenate([z[i][:, :LANES], z[i][:, LANES:] * masks[hh]], axis=1) for i, (_, hh) in enumerate(heads)]
    bp = [mm(b_rb[i], p12[i]) for i in range(nh)]
    q1h = [r_m[i] - bp[i][:, :LANES] for i in range(nh)]
    q2h = [(av[i][c:] - bp[i][:, LANES:]) * masks[hh] for i, (_, hh) in enumerate(heads)]
    q1 = jnp.concatenate([q1h[2 * p] + q1h[2 * p + 1] for p in range(n_pairs)], axis=1)
    q2 = jnp.concatenate([q2h[2 * p] + q2h[2 * p + 1] for p in range(n_pairs)], axis=1)
    r128 = lax.broadcasted_iota(jnp.int32, (LANES, LANES), 0)
    c128 = lax.broadcasted_iota(jnp.int32, (LANES, LANES), 1)
    same = (r128 >= 64) == (c128 >= 64)
    btp = [mm(sl(b_h, p).T, p12[2 * p] + p12[2 * p + 1]) for p in range(n_pairs)]
    ktv = [mm(sl(k_h, p).T, sl(v, p)) for p in range(n_pairs)]
    e_tot = jnp.exp(tot)
    g = [jnp.where(r128 == c128, sl(e_tot, p), 0.0) - jnp.where(same, btp[p][:, :LANES], 0.0) for p in range(n_pairs)]
    hmat = [jnp.where(same, ktv[p] - btp[p][:, LANES:], 0.0) for p in range(n_pairs)]
    return q1, q2, g, hmat


def _rwchunk_kernel(r_ref, lw_ref, k_ref, v_ref, kk_ref, b_ref, q1_ref, q2_ref, g_ref, h_ref, *, c):
    for j in range(r_ref.shape[0] // c):
        rs = slice(j * c, (j + 1) * c)
        q1, q2, g, hmat = _chunk_math(r_ref[rs, :], lw_ref[rs, :], k_ref[rs, :], v_ref[rs, :], kk_ref[rs, :],
                                      b_ref[rs, :], _mm)
        q1_ref[rs, :] = q1
        q2_ref[rs, :] = q2
        for p in range(len(g)):
            g_ref[j, p] = g[p]
            h_ref[j, p] = hmat[p]


def _rwchunk(r, lw, k, v, kk, b, c, per_step=4):
    t = r.shape[0]
    pairs = RWKV_DIM // LANES
    per_step = min(per_step, t // c)
    blk = pl.BlockSpec((per_step * c, RWKV_DIM), lambda i: (i, 0))
    sq = pl.BlockSpec((per_step, pairs, LANES, LANES), lambda i: (i, 0, 0, 0))
    return pl.pallas_call(
        functools.partial(_rwchunk_kernel, c=c),
        grid=(t // (per_step * c),),
        in_specs=[blk] * 6,
        out_specs=[blk, blk, sq, sq],
        out_shape=[jax.ShapeDtypeStruct((t, RWKV_DIM), F32)] * 2
                  + [jax.ShapeDtypeStruct((t // c, pairs, LANES, LANES), F32)] * 2,
        compiler_params=_params(("arbitrary",)),
        name="rwchunk",
    )(r, lw, k, v, kk, b)


def _rwscan_kernel(q1_ref, q2_ref, g_ref, h_ref, y_ref, s_ref):
    @pl.when(pl.program_id(0) == 0)
    def _():
        s_ref[...] = jnp.zeros_like(s_ref)

    for b in range(s_ref.shape[0]):
        for p in range(RWKV_DIM // LANES):
            sl = slice(p * LANES, (p + 1) * LANES)
            s = s_ref[b, p]
            y_ref[b, :, sl] = _mm3(q1_ref[b, :, sl], s) + q2_ref[b, :, sl]
            s_ref[b, p] = _mm3(g_ref[b, 0, p], s) + h_ref[b, 0, p]


def _rwscan(q1, q2, g, hmat, batch, seq, c):
    t = q1.shape[0]
    per = seq // c
    pairs = RWKV_DIM // LANES
    blk = pl.BlockSpec((batch, c, RWKV_DIM), lambda i: (0, i, 0))
    sq = pl.BlockSpec((batch, 1, pairs, LANES, LANES), lambda i: (0, i, 0, 0, 0))
    seqs = lambda a: a.reshape(batch, seq, RWKV_DIM)
    mats = lambda a: a.reshape(batch, per, pairs, LANES, LANES)
    y = pl.pallas_call(
        _rwscan_kernel,
        grid=(per,),
        in_specs=[blk, blk, sq, sq],
        out_specs=blk,
        out_shape=jax.ShapeDtypeStruct((batch, seq, RWKV_DIM), F32),
        scratch_shapes=[pltpu.VMEM((batch, pairs, LANES, LANES), F32)],
        compiler_params=_params(("arbitrary",)),
        name="rwscan",
    )(seqs(q1), seqs(q2), mats(g), mats(hmat))
    return y.reshape(t, RWKV_DIM)


def _post_kernel(y_ref, r_ref, k_ref, v_ref, g_ref, om_ref, gate_ref, x_ref, mod_ref, rk_ref, lnw_ref, lnb_ref,
                 segm_ref, wom_ref, wor_ref, wout_ref, nf_ref, x1_ref, h2_ref, h2bt_ref):
    segm = segm_ref[...]
    y = y_ref[...]
    mu = _mm_lx(y, segm)
    dv = y - mu
    var = _mm_lx(dv * dv, segm)
    yn = dv * lax.rsqrt(var + GN_EPS) * lnw_ref[...] + lnb_ref[...]
    bonus = _mm_lx(r_ref[...] * k_ref[...] * rk_ref[...], segm) * float(RWKV_HEAD)
    yy = (yn + bonus * v_ref[...]) * g_ref[...]
    o_r = _mm(yy, wor_ref[...])
    o_m = _dg(om_ref[0], wom_ref[...])
    d = o_m.shape[1]
    gm = gate_ref[:, 0:d].astype(F32)
    gr = gate_ref[:, d:2 * d].astype(F32)
    mix = _mm(gm * o_m + gr * o_r, wout_ref[...])
    x1 = x_ref[...] + mod_ref[0, 2:3, :] * mix
    x1_ref[...] = x1
    h2 = _norm_mod(x1, nf_ref[...], mod_ref[0, 3:4, :], mod_ref[0, 4:5, :])
    h2_ref[...] = h2
    h2bt_ref[...] = h2.T.astype(BF16)


def _post(y, r, k, v, g, o_mla, gates, x2, mod, consts, batch, seq, tm):
    t, d = x2.shape
    per = seq // tm
    row = lambda w: pl.BlockSpec((tm, w), lambda b, s: (b * per + s, 0))
    return pl.pallas_call(
        _post_kernel,
        grid=(batch, per),
        in_specs=[row(RWKV_DIM)] * 5
                 + [pl.BlockSpec((1, tm, o_mla.shape[2]), lambda b, s: (b, s, 0)),
                    row(gates.shape[1]), row(d),
                    pl.BlockSpec((1, 8, d), lambda b, s: (b, 0, 0))]
                 + [_full(c.shape) for c in consts],
        out_specs=[row(d), row(d), pl.BlockSpec((d, tm), lambda b, s: (0, b * per + s))],
        out_shape=[jax.ShapeDtypeStruct((t, d), F32), jax.ShapeDtypeStruct((t, d), F32),
                   jax.ShapeDtypeStruct((d, t), BF16)],
        compiler_params=_params(("arbitrary", "arbitrary")),
        name="post",
    )(y, r, k, v, g, o_mla, gates, x2, mod, *consts)


def _cand_pairs(n):
    return [(a, b) for a in range(n) for b in range(n) if (a + 1) * (b + 1) <= n]


def _sort_network(n):
    def merge(lo, hi, r):
        step = r * 2
        if step < hi - lo:
            yield from merge(lo, hi, step)
            yield from merge(lo + r, hi, step)
            yield from [(i, i + r) for i in range(lo + r, hi - r, step)]
        else:
            yield (lo, lo + r)

    def sort(lo, hi):
        if hi - lo >= 1:
            mid = lo + (hi - lo) // 2
            yield from sort(lo, mid)
            yield from sort(mid + 1, hi)
            yield from merge(lo, hi, 1)

    return tuple(sort(0, n - 1))


def _top_values(s, n):
    vals = []
    cur = s
    for i in range(n):
        m = jnp.max(cur, axis=0, keepdims=True)
        vals.append(m)
        if i + 1 < n:
            cur = jnp.where(cur >= m, NEG, cur)
    return vals


def _top_values_sorted(mats, n, together=4):
    tm = mats[0].shape[1]
    problems = [(mi, c0) for mi in range(len(mats)) for c0 in range(0, tm, LANES)]
    pieces = [[[] for _ in range(n)] for _ in mats]
    for p0 in range(0, len(problems), together):
        batch = problems[p0:p0 + together]
        cols = [[mats[mi][k * 8:(k + 1) * 8, c0:c0 + LANES] for k in range(mats[mi].shape[0] // 8)]
                for mi, c0 in batch]
        for i, j in _sort_network(len(cols[0])):
            for c in cols:
                c[i], c[j] = jnp.maximum(c[i], c[j]), jnp.minimum(c[i], c[j])
        for r in range(n):
            left = n - 1 - r
            for c, (mi, _) in zip(cols, batch):
                m = jnp.max(c[0], axis=0, keepdims=True)
                pieces[mi][r].append(m)
                if left:
                    hit = c[0] >= m
                    for k in range(min(len(c), left)):
                        below = c[k + 1] if k + 1 < len(c) else NEG
                        c[k] = jnp.where(hit, below, c[k])
    return [[jnp.concatenate(p, axis=1) for p in per_rank] for per_rank in pieces]


def _count_greater(x, v):
    c = jnp.zeros(x.shape, F32)
    for b, vb in enumerate(v):
        c = jnp.where(vb > x, float(b + 1), c)
    return c


def _route_kernel(h_ref, wqh_ref, wql_ref, k1_ref, k2_ref, cnt_ref, e1_ref, rk_ref, e2_ref, hs_ref, sa_ref, sb_ref):
    nk = TOPK + 1
    pairs = _cand_pairs(nk)
    hs_ref[0], hs_ref[1] = _split(h_ref[...])

    def scores(h, s_ref):
        rows = pl.ds(pl.multiple_of(h * 2 * N_KEYS, 2 * N_KEYS), 2 * N_KEYS)
        wh = wqh_ref[rows, :]
        q = _dg(wh, hs_ref[0], NT) + (_dg(wh, hs_ref[1], NT) + _dg(wql_ref[rows, :], hs_ref[0], NT))
        s_ref[0] = _mm3(k1_ref[...], q[:N_KEYS])
        s_ref[1] = _mm3(k2_ref[...], q[N_KEYS:])

    scores(0, sa_ref)

    def two_heads(i, carry):
        select(2 * i, sa_ref, sb_ref)
        select(2 * i + 1, sb_ref, sa_ref)
        return carry

    def select(h, s_ref, next_ref):
        scores(jnp.minimum(h + 1, PEER_HEADS - 1), next_ref)
        s1 = s_ref[0]
        s2 = s_ref[1]
        v1, v2 = _top_values_sorted([s1, s2], nk)
        pad = [jnp.full_like(v1[0], NEG)] * (-len(pairs) % 8)
        cand = jnp.concatenate([v1[a] + v2[b] for a, b in pairs] + pad, axis=0)
        top = _top_values(cand, nk)
        z = sum(jnp.exp(top[i] - top[0]) for i in range(1, TOPK)) + 1.0
        tau = 0.5 * (top[TOPK - 1] + top[TOPK])
        cnt_ref[h] = _count_greater(tau - s1, v2[:TOPK])
        e1_ref[h] = 0.5 * jnp.exp(s1 - v1[0]) / z
        rk_ref[h] = _count_greater(s2, v2[:TOPK]).astype(BF16)
        e2_ref[h] = jnp.exp(s2 - v2[0]).astype(BF16)

    lax.fori_loop(0, PEER_HEADS // 2, two_heads, 0)


def _route(h2, wq_hi, wq_lo, k1, k2, tm):
    t, d = h2.shape
    blk = pl.BlockSpec((PEER_HEADS, N_KEYS, tm), lambda i: (0, 0, i))
    wide = jax.ShapeDtypeStruct((PEER_HEADS, N_KEYS, t), F32)
    half = jax.ShapeDtypeStruct((PEER_HEADS, N_KEYS, t), BF16)
    return pl.pallas_call(
        _route_kernel,
        grid=(t // tm,),
        in_specs=[pl.BlockSpec((tm, d), lambda i: (i, 0)), _full(wq_hi.shape), _full(wq_lo.shape),
                  _full(k1.shape), _full(k2.shape)],
        out_specs=[blk] * 4,
        out_shape=[wide, wide, half, half],
        scratch_shapes=[pltpu.VMEM((2, tm, d), BF16)] + [pltpu.VMEM((2, N_KEYS, tm), F32)] * 2,
        compiler_params=_params(("arbitrary",)),
        name="route",
    )(h2, wq_hi, wq_lo, k1, k2)


def _gelu2(x):
    return x * (1.0 + jnp.tanh(x * (0.7978845608028654 + 0.035677408136300125 * (x * x))))


def _bf16_rows(row):
    tile = jnp.broadcast_to(row, (16, row.shape[1])).astype(BF16)
    return jnp.concatenate([tile] * (N_KEYS // 16), axis=0)


def _experts_kernel(ht_ref, dn_ref, up_ref, cnt_ref, e1_ref, rk_ref, e2_ref, x1_ref, mod_ref, o_ref, acc_ref, ga_ref,
                    *, rows, sub):
    e = pl.program_id(1)
    nb = pl.num_programs(1) - 1
    zero = jnp.zeros((), BF16)
    width = sub * N_KEYS

    @pl.when(e == 0)
    def _():
        acc_ref[...] = jnp.zeros_like(acc_ref)
        ga_ref[1] = jnp.zeros(ga_ref.shape[1:], BF16)

    nsub = rows // sub
    d = acc_ref.shape[0]

    def project_previous(part, parts):
        rs = slice(part * (d // parts), (part + 1) * (d // parts))
        acc_ref[rs, :] += _dg(up_ref[rs, :], ga_ref[(e + 1) % 2])

    @pl.when(e < nb)
    def _():
        raw = {}
        for s in range(nsub + 1):
            if s < nsub:
                raw[s] = _dg(dn_ref[s * width:(s + 1) * width, :], ht_ref[...])
            if s >= 1:
                project_previous(s - 1, nsub)
                act = _gelu2(raw.pop(s - 1).astype(BF16))
                for ii in range(sub):
                    r0 = (s - 1) * sub + ii
                    i = e * rows + r0
                    gsum = None
                    for h in range(PEER_HEADS):
                        cnt = _bf16_rows(cnt_ref[h, pl.ds(i, 1), :])
                        e1 = _bf16_rows(e1_ref[h, pl.ds(i, 1), :])
                        term = e1 * jnp.where(rk_ref[h] < cnt, e2_ref[h], zero)
                        gsum = term if gsum is None else gsum + term
                    ga_ref[e % 2, r0 * N_KEYS:(r0 + 1) * N_KEYS, :] = gsum * act[ii * N_KEYS:(ii + 1) * N_KEYS]

    @pl.when(e == nb)
    def _():
        project_previous(0, 1)
        o_ref[...] = x1_ref[...] + mod_ref[0, 5:6, :] * acc_ref[...].T


def _experts(h2bt, down, up_t, cnt, e1, rk, e2, x1, mod, seq, tm, rows, sub):
    d, t = h2bt.shape
    ne = down.shape[0]
    eb = rows * N_KEYS
    nb = ne // eb
    per = seq // tm
    rt = pl.BlockSpec((PEER_HEADS, N_KEYS, tm), lambda i, e: (0, 0, i))
    return pl.pallas_call(
        functools.partial(_experts_kernel, rows=rows, sub=sub),
        grid=(t // tm, nb + 1),
        in_specs=[pl.BlockSpec((d, tm), lambda i, e: (0, i)),
                  pl.BlockSpec((eb, d), lambda i, e: (jnp.minimum(e, nb - 1), 0)),
                  pl.BlockSpec((d, eb), lambda i, e: (0, jnp.maximum(e - 1, 0))),
                  rt, rt, rt, rt,
                  pl.BlockSpec((tm, d), lambda i, e: (i, 0)),
                  pl.BlockSpec((1, 8, d), lambda i, e: (i // per, 0, 0))],
        out_specs=pl.BlockSpec((tm, d), lambda i, e: (i, 0)),
        out_shape=jax.ShapeDtypeStruct((t, d), F32),
        scratch_shapes=[pltpu.VMEM((d, tm), F32), pltpu.VMEM((2, eb, tm), BF16)],
        compiler_params=_params(("arbitrary", "arbitrary")),
        name="experts",
    )(h2bt, down, up_t, cnt, e1, rk, e2, x1, mod)


def _slab_consts():
    i = jnp.arange(LANES)
    seg = jnp.where(i < QK_NOPE, 0, jnp.where(i < QK_NOPE + QK_ROPE, 1, 2))
    same = (seg[:, None] == seg[None, :]) & (seg[:, None] < 2)
    width = jnp.where(seg == 0, QK_NOPE, QK_ROPE).astype(F32)
    ind = jnp.where(same, 1.0 / width[None, :], 0.0).astype(BF16)
    half = QK_ROPE // 2
    src = jnp.where(i < QK_NOPE + half, i + half, i - half)
    sign = jnp.where(i < QK_NOPE + half, -1.0, 1.0)
    rot = jnp.where((seg[None, :] == 1) & (i[:, None] == src[None, :]), sign[None, :], 0.0).astype(BF16)
    inv = ROPE_THETA ** (-jnp.arange(half, dtype=F32) / half)
    freq = jnp.zeros((LANES,), F32).at[QK_NOPE:QK_NOPE + QK_ROPE].set(jnp.concatenate([inv, inv]))
    return ind, rot, freq.reshape(1, LANES)


def _pad_slab(g):
    return jnp.zeros((1, LANES), F32).at[0, :g.shape[0]].set(g)


class _Tiles(NamedTuple):
    tokens: int
    chunk: int
    attn_q: int
    attn_k: int
    attn_heads: int
    expert_rows: int
    expert_sub: int


def _tiles(seq):
    return _Tiles(tokens=min(512, seq), chunk=min(128, seq), attn_q=min(2048, seq), attn_k=min(1024, seq),
                  attn_heads=4, expert_rows=16, expert_sub=4)


def kernel(x, c, positions, w_ada, b_ada, norm_mix, w_in, mu_shift, q_a_norm, w_uq, kv_a_norm, w_ukv, q_norm, k_norm,
           w_o_mla, w_decay_up, decay_base, w_aaa_up, aaa_base, w_gate_up, k_k, k_a, r_k, ln_x_w, ln_x_b, w_o_rwkv,
           w_out, norm_ffn, w_query, sub_keys1, sub_keys2, expert_down, expert_up):
    batch, seq, d = x.shape
    t = batch * seq
    depth = w_ada.shape[0]
    tl = _tiles(seq)
    tm, chunk = tl.tokens, tl.chunk
    xs = x.reshape(t, d)
    pos = positions.reshape(t, 1)
    ind, rot, freq = _slab_consts()
    lane512 = jnp.arange(RWKV_DIM) // RWKV_HEAD
    seg_sum = (lane512[:, None] == lane512[None, :]).astype(BF16)
    seg_mean = (seg_sum.astype(F32) / RWKV_HEAD).astype(BF16)

    for l in range(depth):
        wi = w_in[l]
        mla_in = Q_LORA + KV_LORA + QK_ROPE
        kr_cols = jnp.zeros((d, LANES), F32).at[:, QK_NOPE:QK_NOPE + QK_ROPE].set(wi[:, Q_LORA + KV_LORA:mla_in])
        w_cat = jnp.concatenate([wi[:, :Q_LORA + KV_LORA], kr_cols, wi[:, mla_in:]], axis=1).astype(BF16)
        wuq = jnp.pad(w_uq[l].reshape(Q_LORA, MLA_HEADS, QK_NOPE + QK_ROPE),
                      ((0, 0), (0, 0), (0, LANES - QK_NOPE - QK_ROPE))).reshape(Q_LORA, MLA_HEADS * LANES).astype(BF16)
        wkv = w_ukv[l].reshape(KV_LORA, MLA_HEADS, QK_NOPE + V_HEAD)
        zeros = jnp.zeros((KV_LORA, MLA_HEADS, LANES - QK_NOPE), F32)
        wuk = jnp.concatenate([wkv[..., :QK_NOPE], zeros], axis=-1).reshape(KV_LORA, MLA_HEADS * LANES).astype(BF16)
        vv = wkv[..., QK_NOPE:]
        odd = (jnp.arange(MLA_HEADS) % 2 == 1)[None, :, None]
        wuv = jnp.concatenate([jnp.where(odd, 0.0, vv), jnp.where(odd, vv, 0.0)], axis=-1)
        wuv = wuv.reshape(KV_LORA, MLA_HEADS * LANES).astype(BF16)
        row = lambda a: a.reshape(1, -1)

        c8 = jnp.pad(c, ((0, 8 - batch % 8 if batch % 8 else 0), (0, 0)))
        mod = _ada(c8, w_ada[l], b_ada[l])[:batch].reshape(batch, N_MOD, d)
        mod = jnp.pad(mod, ((0, 0), (0, 8 - N_MOD), (0, 0)))

        qa, kva, kr, rw, gates = _inproj(xs, mod, norm_mix[l], w_cat, seq, tm)

        mla_consts = (row(q_a_norm[l]), row(kv_a_norm[l]), wuq, wuk, wuv, _pad_slab(q_norm[l]), _pad_slab(k_norm[l]),
                      freq, ind, rot)
        qh, kh, vh = _mlaprep(qa, kva, kr, pos, mla_consts, batch, seq, tm)
        o_mla = _attention(qh, kh, vh, tl.attn_q, tl.attn_k, tl.attn_heads)

        rw_consts = (row(mu_shift[l]), w_decay_up[l], row(decay_base[l]), w_aaa_up[l], row(aaa_base[l]),
                     w_gate_up[l], row(k_k[l]), row(k_a[l]), seg_sum)
        r, lw, kk_, v, kkn, bb, g = _rwprep(rw, rw_consts, batch, seq, tm)
        q1, q2, gmat, hmat = _rwchunk(r, lw, kk_, v, kkn, bb, chunk)
        y = _rwscan(q1, q2, gmat, hmat, batch, seq, chunk)

        post_consts = (row(r_k[l]), row(ln_x_w[l]), row(ln_x_b[l]), seg_mean, w_o_mla[l].astype(BF16),
                       w_o_rwkv[l].astype(BF16), w_out[l].astype(BF16), row(norm_ffn[l]))
        x1, h2, h2bt = _post(y, r, kk_, v, g, o_mla, gates, xs, mod, post_consts, batch, seq, tm)

        wq_t = w_query[l].T
        wq_hi = wq_t.astype(BF16)
        wq_lo = (wq_t - wq_hi.astype(F32)).astype(BF16)
        cnt, e1, rk, e2 = _route(h2, wq_hi, wq_lo, sub_keys1[l], sub_keys2[l], tm)
        xs = _experts(h2bt, expert_down[l].astype(BF16), expert_up[l].T.astype(BF16), cnt, e1, rk, e2, x1, mod,
                      seq, tm, tl.expert_rows, tl.expert_sub)
    return xs.reshape(batch, seq, d)
```

```python
import functools
from typing import NamedTuple

import jax
import jax.numpy as jnp
from jax import lax
from jax.experimental import pallas as pl
from jax.experimental.pallas import tpu as pltpu

F32 = jnp.float32
BF16 = jnp.bfloat16

EPS = 1e-6
GN_EPS = 64e-5
ROPE_THETA = 10000.0
N_MOD = 6
MLA_HEADS = 8
QK_NOPE = 64
QK_ROPE = 32
V_HEAD = 64
Q_LORA = 256
KV_LORA = 128
RWKV_HEADS = 8
RWKV_HEAD = 64
RWKV_DIM = RWKV_HEADS * RWKV_HEAD
DECAY_LORA = 64
AAA_LORA = 64
GATE_LORA = 128
PEER_HEADS = 8
N_KEYS = 128
TOPK = 16

LANES = 128
VMEM_LIMIT = 56 * 1024 * 1024
NEG = -3.0e38

_ONES_LANE = (V_HEAD, 0)

NN = (((1,), (0,)), ((), ()))
NT = (((1,), (1,)), ((), ()))


def _dg(a, b, dims=NN):
    return lax.dot_general(a, b, dims, preferred_element_type=F32)


def _mm(a, b, dims=NN):
    return _dg(a.astype(BF16), b.astype(BF16), dims)


def _split(a):
    hi = a.astype(BF16)
    lo = (a - hi.astype(F32)).astype(BF16)
    return hi, lo


def _mm3(a, b, dims=NN):
    ah, al = _split(a)
    bh, bl = _split(b)
    return _dg(ah, bh, dims) + (_dg(ah, bl, dims) + _dg(al, bh, dims))


def _mm_lx(a, b_exact, dims=NN):
    ah, al = _split(a)
    return _dg(ah, b_exact, dims) + _dg(al, b_exact, dims)


def _params(sem):
    return pltpu.CompilerParams(dimension_semantics=sem, vmem_limit_bytes=VMEM_LIMIT)


def _full(shape):
    nd = len(shape)
    return pl.BlockSpec(shape, lambda *_: (0,) * nd)


def _ada_kernel(c_ref, w_ref, b_ref, o_ref):
    c = c_ref[...]
    s = c * jax.nn.sigmoid(c)
    o_ref[...] = _mm3(s, w_ref[...]) + b_ref[...]


def _ada(c8, w_ada, b_ada):
    d = c8.shape[1]
    n = w_ada.shape[1]
    bn = 1024
    return pl.pallas_call(
        _ada_kernel,
        grid=(n // bn,),
        in_specs=[_full(c8.shape),
                  pl.BlockSpec((d, bn), lambda j: (0, j)),
                  pl.BlockSpec((1, bn), lambda j: (0, j))],
        out_specs=pl.BlockSpec((c8.shape[0], bn), lambda j: (0, j)),
        out_shape=jax.ShapeDtypeStruct((c8.shape[0], n), F32),
        compiler_params=_params(("arbitrary",)),
        name="ada",
    )(c8, w_ada, b_ada.reshape(1, n))


_SEG = (0, Q_LORA, Q_LORA + KV_LORA, Q_LORA + KV_LORA + LANES)
RWKV_IN = 3 * RWKV_DIM + DECAY_LORA + AAA_LORA + GATE_LORA


def _norm_mod(x, g, shift, scale):
    ms = jnp.mean(x * x, axis=-1, keepdims=True)
    h = x * lax.rsqrt(ms + EPS) * g
    return h * (1.0 + scale) + shift


def _inproj_kernel(x_ref, mod_ref, g_ref, w_ref, qa_ref, kva_ref, kr_ref, rw_ref, gate_ref):
    h = _norm_mod(x_ref[...], g_ref[...], mod_ref[0, 0:1, :], mod_ref[0, 1:2, :])
    hb = h.astype(BF16)
    o0, o1, o2, o3 = _SEG
    o4 = o3 + RWKV_IN
    qa_ref[...] = _dg(hb, w_ref[:, o0:o1])
    kva_ref[...] = _dg(hb, w_ref[:, o1:o2])
    kr_ref[...] = _dg(hb, w_ref[:, o2:o3])
    rw_ref[...] = _dg(hb, w_ref[:, o3:o4])
    gate_ref[...] = jax.nn.sigmoid(_dg(hb, w_ref[:, o4:])).astype(BF16)


def _inproj(x2, mod, norm_mix, w_cat, seq, tm):
    t, d = x2.shape
    per = seq // tm
    n_gate = w_cat.shape[1] - _SEG[3] - RWKV_IN
    row = lambda w: pl.BlockSpec((tm, w), lambda i: (i, 0))
    return pl.pallas_call(
        _inproj_kernel,
        grid=(t // tm,),
        in_specs=[row(d),
                  pl.BlockSpec((1, 8, d), lambda i: (i // per, 0, 0)),
                  _full((1, d)),
                  _full(w_cat.shape)],
        out_specs=[row(Q_LORA), row(KV_LORA), row(LANES), row(RWKV_IN), row(n_gate)],
        out_shape=[jax.ShapeDtypeStruct((t, Q_LORA), F32),
                   jax.ShapeDtypeStruct((t, KV_LORA), F32),
                   jax.ShapeDtypeStruct((t, LANES), F32),
                   jax.ShapeDtypeStruct((t, RWKV_IN), F32),
                   jax.ShapeDtypeStruct((t, n_gate), BF16)],
        compiler_params=_params(("arbitrary",)),
        name="inproj",
    )(x2, mod, norm_mix.reshape(1, d), w_cat)


def _mlaprep_kernel(qa_ref, kva_ref, kr_ref, pos_ref, qan_ref, kvan_ref, wuq_ref, wuk_ref, wuv_ref,
                    qg_ref, kg_ref, freq_ref, ind_ref, rot_ref, q_ref, k_ref, v_ref, *, scale):
    ind = ind_ref[...]
    rot = rot_ref[...]
    ang = pos_ref[...].astype(F32) * freq_ref[...]
    cos = jnp.cos(ang)
    sin = jnp.sin(ang)

    def rms(a, g):
        return a * lax.rsqrt(jnp.mean(a * a, axis=-1, keepdims=True) + EPS) * g

    def head_norm(slab, g):
        ms = _mm_lx(slab * slab, ind)
        return slab * lax.rsqrt(ms + EPS) * g

    def rope(a):
        return a * cos + _dg(a.astype(BF16), rot) * sin

    cq = rms(qa_ref[...], qan_ref[...]).astype(BF16)
    ckv = rms(kva_ref[...], kvan_ref[...]).astype(BF16)
    qg = qg_ref[...]
    kg = kg_ref[...]
    k_pe = rope(head_norm(kr_ref[...], kg))
    for h in range(MLA_HEADS):
        sl = slice(h * LANES, (h + 1) * LANES)
        qh = rope(head_norm(_dg(cq, wuq_ref[:, sl]), qg)) * scale
        q_ref[0, h] = qh.astype(BF16)
        kh = head_norm(_dg(ckv, wuk_ref[:, sl]), kg) + k_pe
        k_ref[0, h] = kh.astype(BF16)
        ones = (lax.broadcasted_iota(jnp.int32, (1, LANES), 1) == _ONES_LANE[h % 2]).astype(F32)
        v_ref[0, h] = (_dg(ckv, wuv_ref[:, sl]) + ones).astype(BF16)


def _mlaprep(qa, kva, kr, pos, consts, batch, seq, tm):
    per = seq // tm
    row = lambda w: pl.BlockSpec((tm, w), lambda b, s: (b * per + s, 0))
    hm = pl.BlockSpec((1, MLA_HEADS, tm, LANES), lambda b, s: (b, 0, s, 0))
    hshape = jax.ShapeDtypeStruct((batch, MLA_HEADS, seq, LANES), BF16)
    return pl.pallas_call(
        functools.partial(_mlaprep_kernel, scale=float((QK_NOPE + QK_ROPE) ** -0.5)),
        grid=(batch, per),
        in_specs=[row(Q_LORA), row(KV_LORA), row(LANES), row(1)] + [_full(c.shape) for c in consts],
        out_specs=[hm, hm, hm],
        out_shape=[hshape, hshape, hshape],
        compiler_params=_params(("arbitrary", "arbitrary")),
        name="mlaprep",
    )(qa, kva, kr, pos, *consts)


def _attn_kernel(q_ref, k_ref, v_ref, o_ref, *, tq, tk):
    qi = pl.program_id(2)
    nh = q_ref.shape[1]
    nd = tq // tk

    def update(hh, m, acc, q, start, mask):
        k = k_ref[0, hh, pl.ds(start, tk), :]
        v = v_ref[0, hh, pl.ds(start, tk), :]
        s = _dg(q, k, NT)
        if mask is not None:
            s = jnp.where(mask, s, NEG)
        m_new = jnp.maximum(m, jnp.max(s, axis=-1, keepdims=True))
        p = jnp.exp((s - m_new).astype(BF16))
        return m_new, jnp.exp(m - m_new) * acc + _dg(p, v)

    def below_diagonal(j, carry):
        start = pl.multiple_of(j * tk, tk)
        return tuple(update(hh, *carry[hh], q_ref[0, hh], start, None) for hh in range(nh))

    one = (jnp.full((tq, 1), NEG, F32), jnp.zeros((tq, LANES), F32))
    carry = lax.fori_loop(0, qi * nd, below_diagonal, (one,) * nh)
    for d in range(nd):
        r0 = d * tk
        row = lax.broadcasted_iota(jnp.int32, (tq - r0, tk), 0)
        col = lax.broadcasted_iota(jnp.int32, (tq - r0, tk), 1)
        start = pl.multiple_of(qi * tq + r0, tk)
        new = []
        for hh in range(nh):
            m, acc = carry[hh]
            m_s, acc_s = update(hh, m[r0:], acc[r0:], q_ref[0, hh, r0:, :], start, col <= row)
            if r0:
                m_s = jnp.concatenate([m[:r0], m_s], axis=0)
                acc_s = jnp.concatenate([acc[:r0], acc_s], axis=0)
            new.append((m_s, acc_s))
        carry = tuple(new)
    accs = [acc for _, acc in carry]
    lane = lax.broadcasted_iota(jnp.int32, (1, LANES), 1)
    for p in range(nh // 2):
        acc0, acc1 = accs[2 * p], accs[2 * p + 1]
        l0 = acc0[:, _ONES_LANE[0]:_ONES_LANE[0] + 1]
        l1 = acc1[:, _ONES_LANE[1]:_ONES_LANE[1] + 1]
        o_ref[0, :, p * LANES:(p + 1) * LANES] = jnp.where(lane < V_HEAD, acc0 / l0, acc1 / l1).astype(BF16)


def _attention(q, k, v, tq, tk, hp):
    batch, heads, seq, _ = q.shape
    return pl.pallas_call(
        functools.partial(_attn_kernel, tq=tq, tk=tk),
        grid=(batch, heads // hp, seq // tq),
        in_specs=[pl.BlockSpec((1, hp, tq, LANES), lambda b, p, i: (b, p, i, 0)),
                  pl.BlockSpec((1, hp, seq, LANES), lambda b, p, i: (b, p, 0, 0)),
                  pl.BlockSpec((1, hp, seq, LANES), lambda b, p, i: (b, p, 0, 0))],
        out_specs=pl.BlockSpec((1, tq, hp // 2 * LANES), lambda b, p, i: (b, i, p)),
        out_shape=jax.ShapeDtypeStruct((batch, seq, heads // 2 * LANES), BF16),
        compiler_params=_params(("arbitrary", "arbitrary", "arbitrary")),
        name="attn",
    )(q, k, v)


def _softplus(z):
    return jnp.maximum(z, 0.0) + jnp.log(1.0 + jnp.exp(-jnp.abs(z)))


def _rwprep_kernel(p_ref, prev_ref, mu_ref, wd_ref, db_ref, wa_ref, ab_ref, wg_ref, kk_ref, ka_ref, seg_ref,
                   r_ref, lw_ref, k_ref, v_ref, kkn_ref, b_ref, g_ref):
    p = p_ref[...]
    first = pl.program_id(1) == 0
    last_prev = jnp.where(first, 0.0, prev_ref[7:8, :])
    rows = lax.broadcasted_iota(jnp.int32, p.shape, 0)
    prev = jnp.where(rows == 0, last_prev, pltpu.roll(p, 1, 0))
    p = p + (prev - p) * mu_ref[...]
    d = RWKV_DIM
    r = p[:, 0:d]
    k = p[:, d:2 * d]
    v = p[:, 2 * d:3 * d]
    o = 3 * d
    wd = p[:, o:o + DECAY_LORA]
    ad = p[:, o + DECAY_LORA:o + DECAY_LORA + AAA_LORA]
    gd = p[:, o + DECAY_LORA + AAA_LORA:]
    w = -_softplus(-(db_ref[...] + _mm3(jnp.tanh(wd), wd_ref[...]))) - 0.5
    a = jax.nn.sigmoid(ab_ref[...] + _mm3(ad, wa_ref[...]))
    kk = k * kk_ref[...]
    n2 = _mm_lx(kk * kk, seg_ref[...])
    kk = kk / jnp.maximum(jnp.sqrt(n2), 1e-12)
    r_ref[...] = r
    lw_ref[...] = -jnp.exp(w)
    k_ref[...] = k * (1.0 + (a - 1.0) * ka_ref[...])
    v_ref[...] = v
    kkn_ref[...] = kk
    b_ref[...] = kk * a
    g_ref[...] = _mm(jax.nn.sigmoid(gd), wg_ref[...])


def _rwprep(rw, consts, batch, seq, tm):
    t = rw.shape[0]
    per = seq // tm
    row = pl.BlockSpec((tm, RWKV_DIM), lambda b, s: (b * per + s, 0))
    shp = jax.ShapeDtypeStruct((t, RWKV_DIM), F32)
    return pl.pallas_call(
        _rwprep_kernel,
        grid=(batch, per),
        in_specs=[pl.BlockSpec((tm, RWKV_IN), lambda b, s: (b * per + s, 0)),
                  pl.BlockSpec((8, RWKV_IN), lambda b, s: (jnp.maximum((b * per + s) * (tm // 8) - 1, 0), 0))]
                 + [_full(c.shape) for c in consts],
        out_specs=[row] * 7,
        out_shape=[shp] * 7,
        compiler_params=_params(("arbitrary", "arbitrary")),
        name="rwprep",
    )(rw, rw, *consts)


def _chunk_math(r, lw, k, v, kk, b, mm):
    c = r.shape[0]
    row = lax.broadcasted_iota(jnp.int32, (c, c), 0)
    col = lax.broadcasted_iota(jnp.int32, (c, c), 1)
    incl = row >= col
    strict = row > col
    tri = incl.astype(BF16)
    l_hi = lw.astype(BF16)
    l_r1 = lw - l_hi.astype(F32)
    l_mid = l_r1.astype(BF16)
    l_lo = (l_r1 - l_mid.astype(F32)).astype(BF16)
    cum = _dg(tri, l_hi) + (_dg(tri, l_mid) + _dg(tri, l_lo))
    tot = cum[c - 1:c, :]
    e_pos = jnp.exp(cum)
    e_neg = jnp.exp(-cum)
    e_rem = jnp.exp(tot - cum)
    kk_t = kk * jnp.exp(cum - lw)
    b_t = b * e_neg
    k_t = k * e_neg
    r_t = r * e_pos
    b_h = b * e_rem
    k_h = k * e_rem
    lane = lax.broadcasted_iota(jnp.int32, (1, LANES), 1)
    masks = [(lane < 64).astype(F32), (lane >= 64).astype(F32)]
    n_pairs = r.shape[1] // LANES
    heads = [(p, hh) for p in range(n_pairs) for hh in range(2)]
    nh = len(heads)
    sl = lambda a, p: a[:, p * LANES:(p + 1) * LANES]

    kk_m = [sl(kk_t, p) * masks[hh] for p, hh in heads]
    r_m = [sl(r_t, p) * masks[hh] for p, hh in heads]
    rhs = [jnp.concatenate([sl(b_t, p), sl(k_t, p)], axis=0) for p in range(n_pairs)]
    big = [mm(jnp.concatenate([kk_m[i], r_m[i]], axis=0), rhs[p], NT) for i, (p, _) in enumerate(heads)]
    a_ab = [jnp.where(strict, g_[:c, :c], 0.0) for g_ in big]
    b_rb = [jnp.where(incl, g_[c:, :c], 0.0) for g_ in big]
    av = [mm(jnp.concatenate([jnp.where(strict, big[i][:c, c:], 0.0), jnp.where(incl, big[i][c:, c:], 0.0)], axis=0),
             sl(v, p)) for i, (p, _) in enumerate(heads)]
    z = [jnp.concatenate([kk_m[i], av[i][:c]], axis=1) for i in range(nh)]
    n = a_ab
    z = [z[i] - mm(n[i], z[i]) for i in range(nh)]
    for _ in range(c.bit_length() - 2):
        n = [mm(a, a) for a in n]
        z = [z[i] + mm(n[i], z[i]) for i in range(nh)]
    p12 = [jnp.concatenate([z[i][:, :LANES], z[i][:, LANES:] * masks[hh]], axis=1) for i, (_, hh) in enumerate(heads)]
    bp = [mm(b_rb[i], p12[i]) for i in range(nh)]
    q1h = [r_m[i] - bp[i][:, :LANES] for i in range(nh)]
    q2h = [(av[i][c:] - bp[i][:, LANES:]) * masks[hh] for i, (_, hh) in enumerate(heads)]
    q1 = jnp.concatenate([q1h[2 * p] + q1h[2 * p + 1] for p in range(n_pairs)], axis=1)
    q2 = jnp.concatenate([q2h[2 * p] + q2h[2 * p + 1] for p in range(n_pairs)], axis=1)
    r128 = lax.broadcasted_iota(jnp.int32, (LANES, LANES), 0)
    c128 = lax.broadcasted_iota(jnp.int32, (LANES, LANES), 1)
    same = (r128 >= 64) == (c128 >= 64)
    btp = [mm(sl(b_h, p).T, p12[2 * p] + p12[2 * p + 1]) for p in range(n_pairs)]
    ktv = [mm(sl(k_h, p).T, sl(v, p)) for p in range(n_pairs)]
    e_tot = jnp.exp(tot)
    g = [jnp.where(r128 == c128, sl(e_tot, p), 0.0) - jnp.where(same, btp[p][:, :LANES], 0.0) for p in range(n_pairs)]
    hmat = [jnp.where(same, ktv[p] - btp[p][:, LANES:], 0.0) for p in range(n_pairs)]
    return q1, q2, g, hmat


def _rwchunk_kernel(r_ref, lw_ref, k_ref, v_ref, kk_ref, b_ref, q1_ref, q2_ref, g_ref, h_ref, *, c):
    for j in range(r_ref.shape[0] // c):
        rs = slice(j * c, (j + 1) * c)
        q1, q2, g, hmat = _chunk_math(r_ref[rs, :], lw_ref[rs, :], k_ref[rs, :], v_ref[rs, :], kk_ref[rs, :],
                                      b_ref[rs, :], _mm)
        q1_ref[rs, :] = q1
        q2_ref[rs, :] = q2
        for p in range(len(g)):
            g_ref[j, p] = g[p]
            h_ref[j, p] = hmat[p]


def _rwchunk(r, lw, k, v, kk, b, c, per_step=4):
    t = r.shape[0]
    pairs = RWKV_DIM // LANES
    per_step = min(per_step, t // c)
    blk = pl.BlockSpec((per_step * c, RWKV_DIM), lambda i: (i, 0))
    sq = pl.BlockSpec((per_step, pairs, LANES, LANES), lambda i: (i, 0, 0, 0))
    return pl.pallas_call(
        functools.partial(_rwchunk_kernel, c=c),
        grid=(t // (per_step * c),),
        in_specs=[blk] * 6,
        out_specs=[blk, blk, sq, sq],
        out_shape=[jax.ShapeDtypeStruct((t, RWKV_DIM), F32)] * 2
                  + [jax.ShapeDtypeStruct((t // c, pairs, LANES, LANES), F32)] * 2,
        compiler_params=_params(("arbitrary",)),
        name="rwchunk",
    )(r, lw, k, v, kk, b)


def _rwscan_kernel(q1_ref, q2_ref, g_ref, h_ref, y_ref, s_ref):
    @pl.when(pl.program_id(0) == 0)
    def _():
        s_ref[...] = jnp.zeros_like(s_ref)

    for b in range(s_ref.shape[0]):
        for p in range(RWKV_DIM // LANES):
            sl = slice(p * LANES, (p + 1) * LANES)
            s = s_ref[b, p]
            y_ref[b, :, sl] = _mm3(q1_ref[b, :, sl], s) + q2_ref[b, :, sl]
            s_ref[b, p] = _mm3(g_ref[b, 0, p], s) + h_ref[b, 0, p]


def _rwscan(q1, q2, g, hmat, batch, seq, c):
    t = q1.shape[0]
    per = seq // c
    pairs = RWKV_DIM // LANES
    blk = pl.BlockSpec((batch, c, RWKV_DIM), lambda i: (0, i, 0))
    sq = pl.BlockSpec((batch, 1, pairs, LANES, LANES), lambda i: (0, i, 0, 0, 0))
    seqs = lambda a: a.reshape(batch, seq, RWKV_DIM)
    mats = lambda a: a.reshape(batch, per, pairs, LANES, LANES)
    y = pl.pallas_call(
        _rwscan_kernel,
        grid=(per,),
        in_specs=[blk, blk, sq, sq],
        out_specs=blk,
        out_shape=jax.ShapeDtypeStruct((batch, seq, RWKV_DIM), F32),
        scratch_shapes=[pltpu.VMEM((batch, pairs, LANES, LANES), F32)],
        compiler_params=_params(("arbitrary",)),
        name="rwscan",
    )(seqs(q1), seqs(q2), mats(g), mats(hmat))
    return y.reshape(t, RWKV_DIM)


def _post_kernel(y_ref, r_ref, k_ref, v_ref, g_ref, om_ref, gate_ref, x_ref, mod_ref, rk_ref, lnw_ref, lnb_ref,
                 segm_ref, wom_ref, wor_ref, wout_ref, nf_ref, x1_ref, h2_ref, h2bt_ref):
    segm = segm_ref[...]
    y = y_ref[...]
    mu = _mm_lx(y, segm)
    dv = y - mu
    var = _mm_lx(dv * dv, segm)
    yn = dv * lax.rsqrt(var + GN_EPS) * lnw_ref[...] + lnb_ref[...]
    bonus = _mm_lx(r_ref[...] * k_ref[...] * rk_ref[...], segm) * float(RWKV_HEAD)
    yy = (yn + bonus * v_ref[...]) * g_ref[...]
    o_r = _mm(yy, wor_ref[...])
    o_m = _dg(om_ref[0], wom_ref[...])
    d = o_m.shape[1]
    gm = gate_ref[:, 0:d].astype(F32)
    gr = gate_ref[:, d:2 * d].astype(F32)
    mix = _mm(gm * o_m + gr * o_r, wout_ref[...])
    x1 = x_ref[...] + mod_ref[0, 2:3, :] * mix
    x1_ref[...] = x1
    h2 = _norm_mod(x1, nf_ref[...], mod_ref[0, 3:4, :], mod_ref[0, 4:5, :])
    h2_ref[...] = h2
    h2bt_ref[...] = h2.T.astype(BF16)


def _post(y, r, k, v, g, o_mla, gates, x2, mod, consts, batch, seq, tm):
    t, d = x2.shape
    per = seq // tm
    row = lambda w: pl.BlockSpec((tm, w), lambda b, s: (b * per + s, 0))
    return pl.pallas_call(
        _post_kernel,
        grid=(batch, per),
        in_specs=[row(RWKV_DIM)] * 5
                 + [pl.BlockSpec((1, tm, o_mla.shape[2]), lambda b, s: (b, s, 0)),
                    row(gates.shape[1]), row(d),
                    pl.BlockSpec((1, 8, d), lambda b, s: (b, 0, 0))]
                 + [_full(c.shape) for c in consts],
        out_specs=[row(d), row(d), pl.BlockSpec((d, tm), lambda b, s: (0, b * per + s))],
        out_shape=[jax.ShapeDtypeStruct((t, d), F32), jax.ShapeDtypeStruct((t, d), F32),
                   jax.ShapeDtypeStruct((d, t), BF16)],
        compiler_params=_params(("arbitrary", "arbitrary")),
        name="post",
    )(y, r, k, v, g, o_mla, gates, x2, mod, *consts)


def _cand_pairs(n):
    return [(a, b) for a in range(n) for b in range(n) if (a + 1) * (b + 1) <= n]


def _sort_network(n):
    def merge(lo, hi, r):
        step = r * 2
        if step < hi - lo:
            yield from merge(lo, hi, step)
            yield from merge(lo + r, hi, step)
            yield from [(i, i + r) for i in range(lo + r, hi - r, step)]
        else:
            yield (lo, lo + r)

    def sort(lo, hi):
        if hi - lo >= 1:
            mid = lo + (hi - lo) // 2
            yield from sort(lo, mid)
            yield from sort(mid + 1, hi)
            yield from merge(lo, hi, 1)

    return tuple(sort(0, n - 1))


def _top_values(s, n):
    vals = []
    cur = s
    for i in range(n):
        m = jnp.max(cur, axis=0, keepdims=True)
        vals.append(m)
        if i + 1 < n:
            cur = jnp.where(cur >= m, NEG, cur)
    return vals


def _top_values_sorted(mats, n, together=4):
    tm = mats[0].shape[1]
    problems = [(mi, c0) for mi in range(len(mats)) for c0 in range(0, tm, LANES)]
    pieces = [[[] for _ in range(n)] for _ in mats]
    for p0 in range(0, len(problems), together):
        batch = problems[p0:p0 + together]
        cols = [[mats[mi][k * 8:(k + 1) * 8, c0:c0 + LANES] for k in range(mats[mi].shape[0] // 8)]
                for mi, c0 in batch]
        for i, j in _sort_network(len(cols[0])):
            for c in cols:
                c[i], c[j] = jnp.maximum(c[i], c[j]), jnp.minimum(c[i], c[j])
        for r in range(n):
            left = n - 1 - r
            for c, (mi, _) in zip(cols, batch):
                m = jnp.max(c[0], axis=0, keepdims=True)
                pieces[mi][r].append(m)
                if left:
                    hit = c[0] >= m
                    for k in range(min(len(c), left)):
                        below = c[k + 1] if k + 1 < len(c) else NEG
                        c[k] = jnp.where(hit, below, c[k])
    return [[jnp.concatenate(p, axis=1) for p in per_rank] for per_rank in pieces]


def _count_greater(x, v):
    c = jnp.zeros(x.shape, F32)
    for b, vb in enumerate(v):
        c = jnp.where(vb > x, float(b + 1), c)
    return c


def _route_kernel(h_ref, wqh_ref, wql_ref, k1_ref, k2_ref, cnt_ref, e1_ref, rk_ref, e2_ref, hs_ref, sa_ref, sb_ref):
    nk = TOPK + 1
    pairs = _cand_pairs(nk)
    hs_ref[0], hs_ref[1] = _split(h_ref[...])

    def scores(h, s_ref):
        rows = pl.ds(pl.multiple_of(h * 2 * N_KEYS, 2 * N_KEYS), 2 * N_KEYS)
        wh = wqh_ref[rows, :]
        q = _dg(wh, hs_ref[0], NT) + (_dg(wh, hs_ref[1], NT) + _dg(wql_ref[rows, :], hs_ref[0], NT))
        s_ref[0] = _mm3(k1_ref[...], q[:N_KEYS])
        s_ref[1] = _mm3(k2_ref[...], q[N_KEYS:])

    scores(0, sa_ref)

    def two_heads(i, carry):
        select(2 * i, sa_ref, sb_ref)
        select(2 * i + 1, sb_ref, sa_ref)
        return carry

    def select(h, s_ref, next_ref):
        scores(jnp.minimum(h + 1, PEER_HEADS - 1), next_ref)
        s1 = s_ref[0]
        s2 = s_ref[1]
        v1, v2 = _top_values_sorted([s1, s2], nk)
        pad = [jnp.full_like(v1[0], NEG)] * (-len(pairs) % 8)
        cand = jnp.concatenate([v1[a] + v2[b] for a, b in pairs] + pad, axis=0)
        top = _top_values(cand, nk)
        z = sum(jnp.exp(top[i] - top[0]) for i in range(1, TOPK)) + 1.0
        tau = 0.5 * (top[TOPK - 1] + top[TOPK])
        cnt_ref[h] = _count_greater(tau - s1, v2[:TOPK])
        e1_ref[h] = 0.5 * jnp.exp(s1 - v1[0]) / z
        rk_ref[h] = _count_greater(s2, v2[:TOPK]).astype(BF16)
        e2_ref[h] = jnp.exp(s2 - v2[0]).astype(BF16)

    lax.fori_loop(0, PEER_HEADS // 2, two_heads, 0)


def _route(h2, wq_hi, wq_lo, k1, k2, tm):
    t, d = h2.shape
    blk = pl.BlockSpec((PEER_HEADS, N_KEYS, tm), lambda i: (0, 0, i))
    wide = jax.ShapeDtypeStruct((PEER_HEADS, N_KEYS, t), F32)
    half = jax.ShapeDtypeStruct((PEER_HEADS, N_KEYS, t), BF16)
    return pl.pallas_call(
        _route_kernel,
        grid=(t // tm,),
        in_specs=[pl.BlockSpec((tm, d), lambda i: (i, 0)), _full(wq_hi.shape), _full(wq_lo.shape),
                  _full(k1.shape), _full(k2.shape)],
        out_specs=[blk] * 4,
        out_shape=[wide, wide, half, half],
        scratch_shapes=[pltpu.VMEM((2, tm, d), BF16)] + [pltpu.VMEM((2, N_KEYS, tm), F32)] * 2,
        compiler_params=_params(("arbitrary",)),
        name="route",
    )(h2, wq_hi, wq_lo, k1, k2)


def _gelu2(x):
    return x * (1.0 + jnp.tanh(x * (0.7978845608028654 + 0.035677408136300125 * (x * x))))


def _bf16_rows(row):
    tile = jnp.broadcast_to(row, (16, row.shape[1])).astype(BF16)
    return jnp.concatenate([tile] * (N_KEYS // 16), axis=0)


def _experts_kernel(ht_ref, dn_ref, up_ref, cnt_ref, e1_ref, rk_ref, e2_ref, x1_ref, mod_ref, o_ref, acc_ref, ga_ref,
                    *, rows, sub):
    e = pl.program_id(1)
    nb = pl.num_programs(1) - 1
    zero = jnp.zeros((), BF16)
    width = sub * N_KEYS

    @pl.when(e == 0)
    def _():
        acc_ref[...] = jnp.zeros_like(acc_ref)
        ga_ref[1] = jnp.zeros(ga_ref.shape[1:], BF16)

    nsub = rows // sub
    d = acc_ref.shape[0]

    def project_previous(part, parts):
        rs = slice(part * (d // parts), (part + 1) * (d // parts))
        acc_ref[rs, :] += _dg(up_ref[rs, :], ga_ref[(e + 1) % 2])

    @pl.when(e < nb)
    def _():
        raw = {}
        for s in range(nsub + 1):
            if s < nsub:
                raw[s] = _dg(dn_ref[s * width:(s + 1) * width, :], ht_ref[...])
            if s >= 1:
                project_previous(s - 1, nsub)
                act = _gelu2(raw.pop(s - 1).astype(BF16))
                hw = act.shape[1] // 2
                for ii in range(sub):
                    r0 = (s - 1) * sub + ii
                    i = e * rows + r0
                    tiles = [(_bf16_rows(cnt_ref[h, pl.ds(i, 1), :]), _bf16_rows(e1_ref[h, pl.ds(i, 1), :]))
                             for h in range(PEER_HEADS)]
                    for half in range(2):
                        cs = slice(half * hw, (half + 1) * hw)
                        gsum = None
                        for h in range(PEER_HEADS):
                            cnt, e1 = tiles[h]
                            term = e1[:, cs] * jnp.where(rk_ref[h, :, cs] < cnt[:, cs], e2_ref[h, :, cs], zero)
                            gsum = term if gsum is None else gsum + term
                        ga_ref[e % 2, r0 * N_KEYS:(r0 + 1) * N_KEYS, cs] = (
                            gsum * act[ii * N_KEYS:(ii + 1) * N_KEYS, cs])

    @pl.when(e == nb)
    def _():
        project_previous(0, 1)
        o_ref[...] = x1_ref[...] + mod_ref[0, 5:6, :] * acc_ref[...].T


def _experts(h2bt, down, up_t, cnt, e1, rk, e2, x1, mod, seq, tm, rows, sub):
    d, t = h2bt.shape
    ne = down.shape[0]
    eb = rows * N_KEYS
    nb = ne // eb
    per = seq // tm
    rt = pl.BlockSpec((PEER_HEADS, N_KEYS, tm), lambda i, e: (0, 0, i))
    return pl.pallas_call(
        functools.partial(_experts_kernel, rows=rows, sub=sub),
        grid=(t // tm, nb + 1),
        in_specs=[pl.BlockSpec((d, tm), lambda i, e: (0, i)),
                  pl.BlockSpec((eb, d), lambda i, e: (jnp.minimum(e, nb - 1), 0)),
                  pl.BlockSpec((d, eb), lambda i, e: (0, jnp.maximum(e - 1, 0))),
                  rt, rt, rt, rt,
                  pl.BlockSpec((tm, d), lambda i, e: (i, 0)),
                  pl.BlockSpec((1, 8, d), lambda i, e: (i // per, 0, 0))],
        out_specs=pl.BlockSpec((tm, d), lambda i, e: (i, 0)),
        out_shape=jax.ShapeDtypeStruct((t, d), F32),
        scratch_shapes=[pltpu.VMEM((d, tm), F32), pltpu.VMEM((2, eb, tm), BF16)],
        compiler_params=_params(("arbitrary", "arbitrary")),
        name="experts",
    )(h2bt, down, up_t, cnt, e1, rk, e2, x1, mod)


def _slab_consts():
    i = jnp.arange(LANES)
    seg = jnp.where(i < QK_NOPE, 0, jnp.where(i < QK_NOPE + QK_ROPE, 1, 2))
    same = (seg[:, None] == seg[None, :]) & (seg[:, None] < 2)
    width = jnp.where(seg == 0, QK_NOPE, QK_ROPE).astype(F32)
    ind = jnp.where(same, 1.0 / width[None, :], 0.0).astype(BF16)
    half = QK_ROPE // 2
    src = jnp.where(i < QK_NOPE + half, i + half, i - half)
    sign = jnp.where(i < QK_NOPE + half, -1.0, 1.0)
    rot = jnp.where((seg[None, :] == 1) & (i[:, None] == src[None, :]), sign[None, :], 0.0).astype(BF16)
    inv = ROPE_THETA ** (-jnp.arange(half, dtype=F32) / half)
    freq = jnp.zeros((LANES,), F32).at[QK_NOPE:QK_NOPE + QK_ROPE].set(jnp.concatenate([inv, inv]))
    return ind, rot, freq.reshape(1, LANES)


def _pad_slab(g):
    return jnp.zeros((1, LANES), F32).at[0, :g.shape[0]].set(g)


class _Tiles(NamedTuple):
    tokens: int
    chunk: int
    attn_q: int
    attn_k: int
    attn_heads: int
    expert_rows: int
    expert_sub: int


def _tiles(seq):
    return _Tiles(tokens=min(512, seq), chunk=min(128, seq), attn_q=min(2048, seq), attn_k=min(1024, seq),
                  attn_heads=4, expert_rows=16, expert_sub=4)


def kernel(x, c, positions, w_ada, b_ada, norm_mix, w_in, mu_shift, q_a_norm, w_uq, kv_a_norm, w_ukv, q_norm, k_norm,
           w_o_mla, w_decay_up, decay_base, w_aaa_up, aaa_base, w_gate_up, k_k, k_a, r_k, ln_x_w, ln_x_b, w_o_rwkv,
           w_out, norm_ffn, w_query, sub_keys1, sub_keys2, expert_down, expert_up):
    batch, seq, d = x.shape
    t = batch * seq
    depth = w_ada.shape[0]
    tl = _tiles(seq)
    tm, chunk = tl.tokens, tl.chunk
    xs = x.reshape(t, d)
    pos = positions.reshape(t, 1)
    ind, rot, freq = _slab_consts()
    lane512 = jnp.arange(RWKV_DIM) // RWKV_HEAD
    seg_sum = (lane512[:, None] == lane512[None, :]).astype(BF16)
    seg_mean = (seg_sum.astype(F32) / RWKV_HEAD).astype(BF16)

    for l in range(depth):
        wi = w_in[l]
        mla_in = Q_LORA + KV_LORA + QK_ROPE
        kr_cols = jnp.zeros((d, LANES), F32).at[:, QK_NOPE:QK_NOPE + QK_ROPE].set(wi[:, Q_LORA + KV_LORA:mla_in])
        w_cat = jnp.concatenate([wi[:, :Q_LORA + KV_LORA], kr_cols, wi[:, mla_in:]], axis=1).astype(BF16)
        wuq = jnp.pad(w_uq[l].reshape(Q_LORA, MLA_HEADS, QK_NOPE + QK_ROPE),
                      ((0, 0), (0, 0), (0, LANES - QK_NOPE - QK_ROPE))).reshape(Q_LORA, MLA_HEADS * LANES).astype(BF16)
        wkv = w_ukv[l].reshape(KV_LORA, MLA_HEADS, QK_NOPE + V_HEAD)
        zeros = jnp.zeros((KV_LORA, MLA_HEADS, LANES - QK_NOPE), F32)
        wuk = jnp.concatenate([wkv[..., :QK_NOPE], zeros], axis=-1).reshape(KV_LORA, MLA_HEADS * LANES).astype(BF16)
        vv = wkv[..., QK_NOPE:]
        odd = (jnp.arange(MLA_HEADS) % 2 == 1)[None, :, None]
        wuv = jnp.concatenate([jnp.where(odd, 0.0, vv), jnp.where(odd, vv, 0.0)], axis=-1)
        wuv = wuv.reshape(KV_LORA, MLA_HEADS * LANES).astype(BF16)
        row = lambda a: a.reshape(1, -1)

        c8 = jnp.pad(c, ((0, 8 - batch % 8 if batch % 8 else 0), (0, 0)))
        mod = _ada(c8, w_ada[l], b_ada[l])[:batch].reshape(batch, N_MOD, d)
        mod = jnp.pad(mod, ((0, 0), (0, 8 - N_MOD), (0, 0)))

        qa, kva, kr, rw, gates = _inproj(xs, mod, norm_mix[l], w_cat, seq, tm)

        mla_consts = (row(q_a_norm[l]), row(kv_a_norm[l]), wuq, wuk, wuv, _pad_slab(q_norm[l]), _pad_slab(k_norm[l]),
                      freq, ind, rot)
        qh, kh, vh = _mlaprep(qa, kva, kr, pos, mla_consts, batch, seq, tm)
        o_mla = _attention(qh, kh, vh, tl.attn_q, tl.attn_k, tl.attn_heads)

        rw_consts = (row(mu_shift[l]), w_decay_up[l], row(decay_base[l]), w_aaa_up[l], row(aaa_base[l]),
                     w_gate_up[l], row(k_k[l]), row(k_a[l]), seg_sum)
        r, lw, kk_, v, kkn, bb, g = _rwprep(rw, rw_consts, batch, seq, tm)
        q1, q2, gmat, hmat = _rwchunk(r, lw, kk_, v, kkn, bb, chunk)
        y = _rwscan(q1, q2, gmat, hmat, batch, seq, chunk)

        post_consts = (row(r_k[l]), row(ln_x_w[l]), row(ln_x_b[l]), seg_mean, w_o_mla[l].astype(BF16),
                       w_o_rwkv[l].astype(BF16), w_out[l].astype(BF16), row(norm_ffn[l]))
        x1, h2, h2bt = _post(y, r, kk_, v, g, o_mla, gates, xs, mod, post_consts, batch, seq, tm)

        wq_t = w_query[l].T
        wq_hi = wq_t.astype(BF16)
        wq_lo = (wq_t - wq_hi.astype(F32)).astype(BF16)
        cnt, e1, rk, e2 = _route(h2, wq_hi, wq_lo, sub_keys1[l], sub_keys2[l], tm)
        xs = _experts(h2bt, expert_down[l].astype(BF16), expert_up[l].T.astype(BF16), cnt, e1, rk, e2, x1, mod,
                      seq, tm, tl.expert_rows, tl.expert_sub)
    return xs.reshape(batch, seq, d)
```
